```python
import math
import jax, jax.numpy as jnp
from jax import lax
import numpy as np

D_MODEL = 1024
BATCH = 8
SEQ = 2048
DEPTH = 2

GRID_W = 64
CTX_LEN = 256
N_EVEN = (DEPTH + 1) // 2
N_ODD = DEPTH // 2
MIX_W = D_MODEL
EPS = 1e-6
DA_HEADS = 4
DA_V_DIM = (MIX_W // 2) // DA_HEADS
DA_QK_DIM = DA_V_DIM // 2
DA_QK_W = DA_HEADS * 2 * DA_QK_DIM
DA_V_W = DA_HEADS * DA_V_DIM
Q_BLOCK = 128
ROPE_THETA = 10000.0
POOL_WINDOWS = (2, 4, 8, 16)
POOL_W = MIX_W // 2
POOL_GDIM = POOL_W // len(POOL_WINDOWS)
EV_IN_W = 2 * DA_QK_W + DA_V_W + POOL_W
HG_HEAD_DIM = 128
HG_HEADS = MIX_W // HG_HEAD_DIM
HG_CHUNK = 64
HG_IN_W = 5 * MIX_W
D_FF = 2816
CONV_W = 3

kernel_name = "hybrid_diffattn_pool_hgrn2_dit_block"


def rmsnorm(x):
    x32 = x.astype(jnp.float32)
    y = x32 * lax.rsqrt(jnp.mean(x32 * x32, axis=-1, keepdims=True) + EPS)
    return y.astype(x.dtype)


def modulate(h, shift, scale):
    return h * (1.0 + scale) + shift


def axial_rope_tables(rows_n):
    row = jnp.repeat(jnp.arange(rows_n, dtype=jnp.float32), GRID_W)
    col = jnp.tile(jnp.arange(GRID_W, dtype=jnp.float32), rows_n)
    n_freq = DA_QK_DIM // 4
    inv = ROPE_THETA ** (-jnp.arange(n_freq, dtype=jnp.float32) / n_freq)
    ang = jnp.stack([row[:, None] * inv, col[:, None] * inv], axis=1)
    return jnp.cos(ang), jnp.sin(ang)


def apply_axial_rope(x, cos, sin):
    xr = x.reshape(x.shape[:-1] + (2, 2, DA_QK_DIM // 4))
    x1, x2 = xr[..., 0, :], xr[..., 1, :]
    c = cos[:, None, None].astype(x.dtype)
    s = sin[:, None, None].astype(x.dtype)
    out = jnp.stack([x1 * c - x2 * s, x2 * c + x1 * s], axis=-2)
    return out.reshape(x.shape)


def diff_attention(q, k, v, lam):
    B, Tq, H, M, dq = q.shape
    dv = v.shape[-1]
    nb = Tq // Q_BLOCK
    qb = q.reshape(B, nb, Q_BLOCK, H, M, dq).transpose(1, 0, 2, 3, 4, 5)
    scale = dq ** -0.5

    def one_block(qblk):
        s = jnp.einsum('bqhmd,bkhmd->bhmqk', qblk, k, preferred_element_type=jnp.float32) * scale
        p = jax.nn.softmax(s, axis=-1)
        a = p[:, :, 0] - lam * p[:, :, 1]
        return jnp.einsum('bhqk,bkhd->bqhd', a.astype(v.dtype), v)

    o = lax.map(one_block, qb)
    return o.transpose(1, 0, 2, 3, 4).reshape(B, Tq, H, dv)


def centred_mean_minus_self(u, w):
    B, T, C = u.shape
    cs = jnp.concatenate([jnp.zeros((B, 1, C), jnp.float32), jnp.cumsum(u.astype(jnp.float32), axis=1)], axis=1)
    t = jnp.arange(T)
    lo = jnp.clip(t - w // 2, 0, T)
    hi = jnp.clip(t + w - w // 2, 0, T)
    cnt = (hi - lo).astype(jnp.float32)
    mean = (cs[:, hi] - cs[:, lo]) / cnt[None, :, None]
    return mean.astype(u.dtype) - u


def pool_mixer(u, pool_w, pool_scale):
    B, T, _ = u.shape
    d = jnp.stack([centred_mean_minus_self(u[..., g * POOL_GDIM:(g + 1) * POOL_GDIM], w)
                   for g, w in enumerate(POOL_WINDOWS)], axis=2)
    y = jnp.einsum('btgc,gcd->btgd', d, pool_w).reshape(B, T, POOL_W)
    return y * pool_scale


def even_mixer(h_ctx, h_lat, w_in, w_out, lq1, lk1, lq2, lk2, subln, pool_w, pool_scale,
               lambda_init, cos, sin, need_ctx):
    def project(h):
        B, T, _ = h.shape
        p = h @ w_in
        q = p[..., :DA_QK_W].reshape(B, T, DA_HEADS, 2, DA_QK_DIM)
        k = p[..., DA_QK_W:2 * DA_QK_W].reshape(B, T, DA_HEADS, 2, DA_QK_DIM)
        v = p[..., 2 * DA_QK_W:2 * DA_QK_W + DA_V_W].reshape(B, T, DA_HEADS, DA_V_DIM)
        u = p[..., 2 * DA_QK_W + DA_V_W:]
        return q, k, v, u

    qc, kc, vc, uc = project(h_ctx)
    ql, kl, vl, ul = project(h_lat)
    ql = apply_axial_rope(ql, cos, sin)
    kl = apply_axial_rope(kl, cos, sin)
    lam = (jnp.exp(jnp.sum(lq1.astype(jnp.float32) * lk1.astype(jnp.float32)))
           - jnp.exp(jnp.sum(lq2.astype(jnp.float32) * lk2.astype(jnp.float32))) + lambda_init)
    gain = subln.reshape(DA_HEADS, DA_V_DIM) * (1.0 - lambda_init)

    def readout(o, u):
        B, T = o.shape[:2]
        o = (rmsnorm(o) * gain).reshape(B, T, DA_V_W)
        return jnp.concatenate([o, pool_mixer(u, pool_w, pool_scale)], axis=-1) @ w_out

    k_all = jnp.concatenate([kc, kl], axis=1)
    v_all = jnp.concatenate([vc, vl], axis=1)
    y_lat = readout(diff_attention(ql, k_all, v_all, lam), ul)
    y_ctx = readout(diff_attention(qc, kc, vc, lam), uc) if need_ctx else None
    return y_ctx, y_lat


def hgrn2_chunk_scan(q, logf, k, v, s0):
    B, T, H, dk = q.shape
    dv = v.shape[-1]
    n = T // HG_CHUNK

    def to_chunks(a):
        return a.reshape(B, n, HG_CHUNK, H, a.shape[-1]).transpose(1, 0, 3, 2, 4)

    mask = jnp.tril(jnp.ones((HG_CHUNK, HG_CHUNK), dtype=bool))[None, None, :, :, None]

    def step(S, xs):
        qc, gc, kc, vc = xs
        b = jnp.cumsum(gc, axis=2)
        diff = b[:, :, :, None, :] - b[:, :, None, :, :]
        decay = jnp.exp(jnp.where(mask, diff, -jnp.inf))
        A = jnp.einsum('bhtc,bhtsc,bhsc->bhts', qc, decay, kc)
        o = jnp.einsum('bhts,bhsd->bhtd', A, vc) + jnp.einsum('bhtc,bhcd->bhtd', qc * jnp.exp(b), S)
        b_last = b[:, :, -1:, :]
        S_new = (jnp.exp(b_last[:, :, 0, :])[..., None] * S
                 + jnp.einsum('bhsc,bhsd->bhcd', kc * jnp.exp(b_last - b), vc))
        return S_new, o

    S_fin, o = lax.scan(step, s0, (to_chunks(q), to_chunks(logf), to_chunks(k), to_chunks(v)))
    o = o.transpose(1, 0, 3, 2, 4).reshape(B, T, H, dv)
    return o, S_fin


def hgrn2_direction(q, z, v, lb, s0, reverse):
    f = lb + (1.0 - lb) * jax.nn.sigmoid(z.astype(jnp.float32))
    k = 1.0 - f
    logf = jnp.log(f)
    xs = (q.astype(jnp.float32), logf, k, v.astype(jnp.float32))
    if reverse:
        xs = tuple(jnp.flip(a, axis=1) for a in xs)
    o, s = hgrn2_chunk_scan(xs[0], xs[1], xs[2], xs[3], s0)
    if reverse:
        o = jnp.flip(o, axis=1)
    return o.astype(v.dtype), s


def odd_mixer(h_ctx, h_lat, w_in, w_out, lb_f, lb_b, norm_g, need_ctx):
    def project(h):
        B, T, _ = h.shape
        p = (h @ w_in).reshape(B, T, 5, HG_HEADS, HG_HEAD_DIM)
        return jax.nn.silu(p[:, :, 0]), p[:, :, 1], p[:, :, 2], p[:, :, 3], p[:, :, 4]

    qc, vc, zfc, zbc, gc = project(h_ctx)
    ql, vl, zfl, zbl, gl = project(h_lat)
    lbf = lb_f.reshape(HG_HEADS, HG_HEAD_DIM)
    lbb = lb_b.reshape(HG_HEADS, HG_HEAD_DIM)
    B = h_lat.shape[0]
    s0 = jnp.zeros((B, HG_HEADS, HG_HEAD_DIM, HG_HEAD_DIM), jnp.float32)
    oc_f, s_f = hgrn2_direction(qc, zfc, vc, lbf, s0, False)
    oc_b, s_b = hgrn2_direction(qc, zbc, vc, lbb, s0, True)
    ol_f, _ = hgrn2_direction(ql, zfl, vl, lbf, s_f, False)
    ol_b, _ = hgrn2_direction(ql, zbl, vl, lbb, s_b, True)
    gain = norm_g.reshape(HG_HEADS, HG_HEAD_DIM)

    def readout(o, g):
        Bq, T = o.shape[:2]
        o = rmsnorm(o) * gain * jax.nn.silu(g)
        return o.reshape(Bq, T, MIX_W) @ w_out

    y_lat = readout(ol_f + ol_b, gl)
    y_ctx = readout(oc_f + oc_b, gc) if need_ctx else None
    return y_ctx, y_lat


def conv_ffn(h, w_up, conv_w, conv_b, w_down):
    u = h @ w_up
    up = jnp.pad(u, ((0, 0), (1, 1), (0, 0)))
    u = up[:, :-2] * conv_w[0] + up[:, 1:-1] * conv_w[1] + up[:, 2:] * conv_w[2] + conv_b
    a, gate = jnp.split(u, 2, axis=-1)
    return (a * jax.nn.silu(gate)) @ w_down


def setup_inputs(seed: int = 0) -> dict:
    key = jax.random.key(seed)
    ks = jax.random.split(key, 26)
    f32 = jnp.float32

    def nrm(k, shape, scale):
        return jax.random.normal(k, shape, f32) * scale

    D = D_MODEL
    return {
        "x": nrm(ks[0], (BATCH, SEQ, D), 1.0),
        "c": nrm(ks[1], (BATCH, D), 1.0),
        "ctx": nrm(ks[2], (BATCH, CTX_LEN, D), 1.0),
        "c_ctx": nrm(ks[3], (D,), 1.0),
        "w_mod": nrm(ks[4], (DEPTH, D, 6 * D), 0.5 * D ** -0.5),
        "b_mod": nrm(ks[5], (DEPTH, 6 * D), 0.02),
        "ev_w_in": nrm(ks[6], (N_EVEN, D, EV_IN_W), D ** -0.5),
        "ev_w_out": nrm(ks[7], (N_EVEN, DA_V_W + POOL_W, D), (DA_V_W + POOL_W) ** -0.5),
        "da_lq1": nrm(ks[8], (N_EVEN, DA_QK_DIM), 0.1),
        "da_lk1": nrm(ks[9], (N_EVEN, DA_QK_DIM), 0.1),
        "da_lq2": nrm(ks[10], (N_EVEN, DA_QK_DIM), 0.1),
        "da_lk2": nrm(ks[11], (N_EVEN, DA_QK_DIM), 0.1),
        "da_subln": 1.0 + nrm(ks[12], (N_EVEN, DA_V_W), 0.02),
        "pool_w": nrm(ks[13], (N_EVEN, len(POOL_WINDOWS), POOL_GDIM, POOL_GDIM), POOL_GDIM ** -0.5),
        "pool_scale": 1.0 + nrm(ks[14], (N_EVEN, POOL_W), 0.02),
        "hg_w_in": nrm(ks[15], (N_ODD, D, HG_IN_W), D ** -0.5),
        "hg_w_out": nrm(ks[16], (N_ODD, MIX_W, D), MIX_W ** -0.5),
        "hg_lb_logits": nrm(ks[17], (2, DEPTH, MIX_W), 0.5),
        "hg_norm": 1.0 + nrm(ks[18], (N_ODD, MIX_W), 0.02),
        "ffn_w_up": nrm(ks[19], (DEPTH, D, 2 * D_FF), D ** -0.5),
        "ffn_conv_w": nrm(ks[20], (DEPTH, CONV_W, 2 * D_FF), CONV_W ** -0.5),
        "ffn_conv_b": nrm(ks[21], (DEPTH, 2 * D_FF), 0.02),
        "ffn_w_down": nrm(ks[22], (DEPTH, D_FF, D), D_FF ** -0.5),
        "final_norm": 1.0 + nrm(ks[23], (D,), 0.02),
    }


def reference(x, c, ctx, c_ctx, w_mod, b_mod, ev_w_in, ev_w_out, da_lq1, da_lk1, da_lq2, da_lk2,
              da_subln, pool_w, pool_scale, hg_w_in, hg_w_out, hg_lb_logits, hg_norm,
              ffn_w_up, ffn_conv_w, ffn_conv_b, ffn_w_down, final_norm):
    B, T, D = x.shape
    ROWS = T // GRID_W
    cos, sin = axial_rope_tables(ROWS)
    lb_p = jax.nn.softmax(hg_lb_logits.astype(jnp.float32), axis=1)
    lb = jnp.cumsum(lb_p, axis=1)
    lb = lb - lb[:, :1]
    xc = ctx
    for i in range(DEPTH):
        need_ctx = i < DEPTH - 1
        m_lat = (jax.nn.silu(c) @ w_mod[i] + b_mod[i]).reshape(B, 6, 1, D)
        m_ctx = (jax.nn.silu(c_ctx) @ w_mod[i] + b_mod[i]).reshape(6, 1, 1, D)
        h_lat = modulate(rmsnorm(x), m_lat[:, 0], m_lat[:, 1])
        h_ctx = modulate(rmsnorm(xc), m_ctx[0], m_ctx[1])
        j = i // 2
        if i % 2 == 0:
            lambda_init = 0.8 - 0.6 * math.exp(-0.3 * i)
            y_ctx, y_lat = even_mixer(h_ctx, h_lat, ev_w_in[j], ev_w_out[j], da_lq1[j], da_lk1[j],
                                      da_lq2[j], da_lk2[j], da_subln[j], pool_w[j], pool_scale[j],
                                      lambda_init, cos, sin, need_ctx)
        else:
            y_ctx, y_lat = odd_mixer(h_ctx, h_lat, hg_w_in[j], hg_w_out[j], lb[0, i], lb[1, i],
                                     hg_norm[j], need_ctx)
        x = x + m_lat[:, 2] * y_lat
        x = x + m_lat[:, 5] * conv_ffn(modulate(rmsnorm(x), m_lat[:, 3], m_lat[:, 4]),
                                       ffn_w_up[i], ffn_conv_w[i], ffn_conv_b[i], ffn_w_down[i])
        if need_ctx:
            xc = xc + m_ctx[2] * y_ctx
            xc = xc + m_ctx[5] * conv_ffn(modulate(rmsnorm(xc), m_ctx[3], m_ctx[4]),
                                          ffn_w_up[i], ffn_conv_w[i], ffn_conv_b[i], ffn_w_down[i])
    return rmsnorm(x) * final_norm
```

```python
import functools
import math

import jax
import jax.numpy as jnp
from jax import lax
from jax.experimental import pallas as pl
from jax.experimental.pallas import tpu as pltpu

F32 = jnp.float32
BF16 = jnp.bfloat16

EPS = 1e-6
GRID_W = 64
DA_HEADS = 4
DA_QK_DIM = 64
ROPE_THETA = 10000.0
POOL_WINDOWS = (2, 4, 8, 16)
HG_HEAD_DIM = 128
LANES = 128
SUBLANES = 8
VMEM_LIMIT = 56 * 1024 * 1024

TM = 256
TQ = 256
HG_CHUNK = 32
HG_LANES = 256
FF_CHUNK = 256


def _resident(shape):
    nd = len(shape)
    return pl.BlockSpec(shape, lambda *_: (0,) * nd, pipeline_mode=pl.Buffered(1))


def _sigmoid(x):
    return 1.0 / (1.0 + jnp.exp(-x))


def _modnorm(x, shift, scale):
    ms = jnp.mean(x * x, axis=-1, keepdims=True)
    return (x * lax.rsqrt(ms + EPS)) * (1.0 + scale) + shift


def _dot(a, b):
    return jnp.dot(a, b, preferred_element_type=F32)


def _dot_nt(a, b):
    return lax.dot_general(a, b, (((1,), (1,)), ((), ())), preferred_element_type=F32)


def _dot_tn(a, b):
    return lax.dot_general(a, b, (((0,), (0,)), ((), ())), preferred_element_type=F32)


def _mod_kernel(cond_ref, w_ref, b_ref, out_ref):
    c = cond_ref[...]
    s = c * _sigmoid(c)
    out_ref[0] = jnp.dot(s, w_ref[0], precision=lax.Precision.HIGHEST,
                         preferred_element_type=F32) + b_ref[0]


def _modulation(cond, w_mod, b_mod):
    depth, d, n = w_mod.shape
    rows = cond.shape[0]
    tn = 1024
    return pl.pallas_call(
        _mod_kernel,
        grid=(depth, n // tn),
        in_specs=[pl.BlockSpec((rows, d), lambda l, j: (0, 0)),
                  pl.BlockSpec((1, d, tn), lambda l, j: (l, 0, j)),
                  pl.BlockSpec((1, 1, tn), lambda l, j: (l, 0, j))],
        out_specs=pl.BlockSpec((1, rows, tn), lambda l, j: (l, 0, j)),
        out_shape=jax.ShapeDtypeStruct((depth, rows, n), F32),
        compiler_params=pltpu.CompilerParams(dimension_semantics=("parallel", "parallel"),
                                             vmem_limit_bytes=VMEM_LIMIT),
        name="adaln_mod",
    )(cond, w_mod, b_mod.reshape(depth, 1, n))


def _mod_row(i, n_lat_tiles, tiles_per_seq, ctx_row):
    return jnp.where(i < n_lat_tiles, i // tiles_per_seq, ctx_row)


def _inproj_even_kernel(x_ref, mod_ref, w_ref, cos_ref, sa_ref, sb_ref, q_ref, k_ref, v_ref, u_ref):
    m = mod_ref[0]
    h = _modnorm(x_ref[...], m[0:1], m[1:2]).astype(BF16)
    w = q_ref.shape[1]
    reps = w // LANES
    cos = jnp.concatenate([cos_ref[...]] * reps, axis=1)
    sa = jnp.concatenate([sa_ref[...]] * reps, axis=1)
    sb = jnp.concatenate([sb_ref[...]] * reps, axis=1)
    half = DA_QK_DIM // 4

    def rope(p):
        return p * cos + pltpu.roll(p, w - half, axis=1) * sa + pltpu.roll(p, half, axis=1) * sb

    q_ref[...] = (rope(_dot(h, w_ref[:, 0:w])) * (DA_QK_DIM ** -0.5)).astype(BF16)
    k_ref[...] = rope(_dot(h, w_ref[:, w:2 * w])).astype(BF16)
    v_ref[...] = _dot(h, w_ref[:, 2 * w:3 * w]).astype(BF16)
    u_ref[...] = _dot(h, w_ref[:, 3 * w:4 * w])


def _inproj_even(xs, mods, w_in, rope_tabs, n_lat_rows, seq, ctx_row):
    rows, d = xs.shape
    w = w_in.shape[1] // 4
    n_lat_tiles = n_lat_rows // TM
    tps = seq // TM
    row = functools.partial(_mod_row, n_lat_tiles=n_lat_tiles, tiles_per_seq=tps, ctx_row=ctx_row)
    tab_spec = pl.BlockSpec((TM, LANES), lambda i: (jnp.where(i < n_lat_tiles, i % tps, tps), 0))
    tok = lambda dt: jax.ShapeDtypeStruct((rows, w), dt)
    return pl.pallas_call(
        _inproj_even_kernel,
        grid=(rows // TM,),
        in_specs=[pl.BlockSpec((TM, d), lambda i: (i, 0)),
                  pl.BlockSpec((1, 6, d), lambda i: (row(i), 0, 0)),
                  _resident(w_in.shape), tab_spec, tab_spec, tab_spec],
        out_specs=[pl.BlockSpec((TM, w), lambda i: (i, 0))] * 4,
        out_shape=[tok(BF16), tok(BF16), tok(BF16), tok(F32)],
        compiler_params=pltpu.CompilerParams(dimension_semantics=("parallel",),
                                             vmem_limit_bytes=VMEM_LIMIT),
        name="even_inproj",
    )(xs, mods, w_in, *rope_tabs)


def _rope_tables(seq):
    rows_n = seq // GRID_W
    row = jnp.repeat(jnp.arange(rows_n, dtype=F32), GRID_W)
    col = jnp.tile(jnp.arange(GRID_W, dtype=F32), rows_n)
    n_freq = DA_QK_DIM // 4
    inv = ROPE_THETA ** (-jnp.arange(n_freq, dtype=F32) / n_freq)
    ang_r, ang_c = row[:, None] * inv, col[:, None] * inv
    zeros = jnp.zeros_like(ang_r)
    cos64 = jnp.concatenate([jnp.cos(ang_r)] * 2 + [jnp.cos(ang_c)] * 2, axis=1)
    sa64 = jnp.concatenate([-jnp.sin(ang_r), zeros, -jnp.sin(ang_c), zeros], axis=1)
    sb64 = jnp.concatenate([zeros, jnp.sin(ang_r), zeros, jnp.sin(ang_c)], axis=1)

    def finish(t, fill):
        t = jnp.concatenate([t, t], axis=1)
        return jnp.concatenate([t, jnp.full((TM, LANES), fill, F32)], axis=0)

    return finish(cos64, 1.0), finish(sa64, 0.0), finish(sb64, 0.0)


def _attn_kernel(*refs, lambda_init, has_lat):
    if has_lat:
        lqk_ref, gain_ref, q_ref, kc_ref, vc_ref, kl_ref, vl_ref, o_ref = refs
    else:
        lqk_ref, gain_ref, q_ref, kc_ref, vc_ref, o_ref = refs
    a = lqk_ref[...]
    lam = (jnp.exp(jnp.sum(a[0:1] * a[1:2], axis=-1, keepdims=True))
           - jnp.exp(jnp.sum(a[2:3] * a[3:4], axis=-1, keepdims=True)) + lambda_init)
    q = q_ref[...]
    lane = lax.broadcasted_iota(jnp.int32, (1, LANES), 1)
    zero = jnp.zeros_like(q)

    def softmax_pv(qm):
        sc = _dot_nt(qm, kc_ref[...])
        m = jnp.max(sc, axis=-1, keepdims=True)
        if has_lat:
            sl = _dot_nt(qm, kl_ref[...])
            m = jnp.maximum(m, jnp.max(sl, axis=-1, keepdims=True))
        ec = jnp.exp(sc - m)
        l = jnp.sum(ec, axis=-1, keepdims=True)
        o = _dot(ec.astype(BF16), vc_ref[...])
        if has_lat:
            el = jnp.exp(sl - m)
            l = l + jnp.sum(el, axis=-1, keepdims=True)
            o = o + _dot(el.astype(BF16), vl_ref[...])
        return o / l

    o = softmax_pv(jnp.where(lane < DA_QK_DIM, q, zero)) - lam * softmax_pv(jnp.where(lane >= DA_QK_DIM, q, zero))
    ms = jnp.mean(o * o, axis=-1, keepdims=True)
    o_ref[...] = (o * lax.rsqrt(ms + EPS) * (gain_ref[...] * (1.0 - lambda_init))).astype(BF16)


def _diff_attention(q, k, v, lqk, subln, lambda_init, batch, seq, ctx_len):
    rows, w = q.shape
    n_lat_rows = batch * seq
    ctx0 = n_lat_rows // ctx_len
    gain = subln.reshape(1, w)
    small = [pl.BlockSpec(lqk.shape, lambda *_: (0, 0)),
             pl.BlockSpec((1, LANES), lambda b, h, *_: (0, h))]
    ctx_kv = pl.BlockSpec((ctx_len, LANES), lambda b, h, *_: (ctx0 + b, h))
    lat_kv = pl.BlockSpec((seq, LANES), lambda b, h, *_: (b, h))
    nq = seq // TQ
    o_lat = pl.pallas_call(
        functools.partial(_attn_kernel, lambda_init=lambda_init, has_lat=True),
        grid=(batch, DA_HEADS, nq),
        in_specs=small + [pl.BlockSpec((TQ, LANES), lambda b, h, i: (b * nq + i, h)),
                          ctx_kv, ctx_kv, lat_kv, lat_kv],
        out_specs=pl.BlockSpec((TQ, LANES), lambda b, h, i: (b * nq + i, h)),
        out_shape=jax.ShapeDtypeStruct((n_lat_rows, w), BF16),
        compiler_params=pltpu.CompilerParams(dimension_semantics=("parallel", "parallel", "parallel"),
                                             vmem_limit_bytes=VMEM_LIMIT),
        name="diff_attn_latent",
    )(lqk, gain, q, k, v, k, v)
    o_ctx = pl.pallas_call(
        functools.partial(_attn_kernel, lambda_init=lambda_init, has_lat=False),
        grid=(batch, DA_HEADS),
        in_specs=small + [ctx_kv, ctx_kv, ctx_kv],
        out_specs=pl.BlockSpec((ctx_len, LANES), lambda b, h: (b, h)),
        out_shape=jax.ShapeDtypeStruct((rows - n_lat_rows, w), BF16),
        compiler_params=pltpu.CompilerParams(dimension_semantics=("parallel", "parallel"),
                                             vmem_limit_bytes=VMEM_LIMIT),
        name="diff_attn_context",
    )(lqk, gain, q, k, v)
    return jnp.concatenate([o_lat, o_ctx], axis=0)


def _pool_kernel(u_ref, pw_ref, ps_ref, out_ref):
    n = u_ref.shape[0]
    t = lax.broadcasted_iota(jnp.int32, (n, 1), 0)
    for g, w in enumerate(POOL_WINDOWS):
        sl = slice(g * LANES, (g + 1) * LANES)
        ug = u_ref[:, sl]
        lo, hi = -(w // 2), w - w // 2
        acc = ug
        for off in range(lo, hi):
            if off == 0:
                continue
            valid = (t + off >= 0) & (t + off < n)
            acc = acc + jnp.where(valid, pltpu.roll(ug, (-off) % n, axis=0), 0.0)
        cnt = (jnp.minimum(t + hi, n) - jnp.maximum(t + lo, 0)).astype(F32)
        dlt = acc / cnt - ug
        y = _dot(dlt.astype(BF16), pw_ref[g]) * ps_ref[:, sl]
        out_ref[:, sl] = y.astype(BF16)


def _pool(u, pool_w, pool_scale, batch, seq, ctx_len):
    rows, w = u.shape
    n_lat_rows = batch * seq
    ctx0 = n_lat_rows // ctx_len

    def call(n, off, name):
        return pl.pallas_call(
            _pool_kernel,
            grid=(batch,),
            in_specs=[pl.BlockSpec((n, w), lambda b: (off + b, 0)),
                      pl.BlockSpec(pool_w.shape, lambda b: (0, 0, 0)),
                      pl.BlockSpec((1, w), lambda b: (0, 0))],
            out_specs=pl.BlockSpec((n, w), lambda b: (b, 0)),
            out_shape=jax.ShapeDtypeStruct((batch * n, w), BF16),
            compiler_params=pltpu.CompilerParams(dimension_semantics=("parallel",),
                                                 vmem_limit_bytes=VMEM_LIMIT),
            name=name,
        )(u, pool_w, pool_scale.reshape(1, w))

    return jnp.concatenate([call(seq, 0, "pool_latent"), call(ctx_len, ctx0, "pool_context")], axis=0)


def _outproj_even_kernel(x_ref, mod_ref, o_ref, p_ref, w_ref, out_ref):
    k = o_ref.shape[1]
    y = _dot(o_ref[...], w_ref[0:k, :]) + _dot(p_ref[...], w_ref[k:, :])
    out_ref[...] = x_ref[...] + mod_ref[0][2:3] * y


def _outproj_even(xs, mods, o, p, w_out, n_lat_rows, seq, ctx_row):
    rows, d = xs.shape
    k = o.shape[1]
    row = functools.partial(_mod_row, n_lat_tiles=n_lat_rows // TM, tiles_per_seq=seq // TM, ctx_row=ctx_row)
    return pl.pallas_call(
        _outproj_even_kernel,
        grid=(rows // TM,),
        in_specs=[pl.BlockSpec((TM, d), lambda i: (i, 0)),
                  pl.BlockSpec((1, 6, d), lambda i: (row(i), 0, 0)),
                  pl.BlockSpec((TM, k), lambda i: (i, 0)),
                  pl.BlockSpec((TM, k), lambda i: (i, 0)),
                  _resident(w_out.shape)],
        out_specs=pl.BlockSpec((TM, d), lambda i: (i, 0)),
        out_shape=jax.ShapeDtypeStruct((rows, d), F32),
        compiler_params=pltpu.CompilerParams(dimension_semantics=("parallel",),
                                             vmem_limit_bytes=VMEM_LIMIT),
        name="even_outproj",
    )(xs, mods, o, p, w_out)


def _ffn_kernel(x_ref, xp_ref, xn_ref, mod_ref, wu_ref, cw_ref, cb_ref, wd_ref, fin_ref, out_ref,
                *, n_lat_tiles, seq, ctx_len, final):
    i = pl.program_id(0)
    tm = x_ref.shape[0]
    d_ff = wd_ref.shape[0]
    length = jnp.where(i < n_lat_tiles, seq, ctx_len)
    at_start = (i * tm) % length == 0
    at_end = ((i + 1) * tm) % length == 0
    m = mod_ref[0]
    shift, scale = m[3:4], m[4:5]
    x = x_ref[...]
    hp = jnp.where(at_start, 0.0, _modnorm(xp_ref[...], shift, scale))
    hn = jnp.where(at_end, 0.0, _modnorm(xn_ref[...], shift, scale))
    h = jnp.concatenate([hp, _modnorm(x, shift, scale), hn], axis=0).astype(BF16)
    ext = tm + 2 * SUBLANES

    def conv(u, c0):
        cw = cw_ref[:, c0:c0 + FF_CHUNK]
        y = (pltpu.roll(u, 1, axis=0) * cw[0:1] + u * cw[1:2] + pltpu.roll(u, ext - 1, axis=0) * cw[2:3]
             + cb_ref[:, c0:c0 + FF_CHUNK])
        return y[SUBLANES:SUBLANES + tm]

    acc = jnp.zeros((tm, out_ref.shape[1]), F32)
    for j in range(d_ff // FF_CHUNK):
        ca, cg = j * FF_CHUNK, d_ff + j * FF_CHUNK
        a = conv(_dot(h, wu_ref[:, ca:ca + FF_CHUNK]), ca)
        g = conv(_dot(h, wu_ref[:, cg:cg + FF_CHUNK]), cg)
        act = (a * (g * _sigmoid(g))).astype(BF16)
        acc = acc + _dot(act, wd_ref[ca:ca + FF_CHUNK, :])
    y = x + m[5:6] * acc
    if final:
        ms = jnp.mean(y * y, axis=-1, keepdims=True)
        y = y * lax.rsqrt(ms + EPS) * fin_ref[...]
    out_ref[...] = y


def _conv_ffn(xs, mods, w_up, conv_w, conv_b, w_down, final_norm, out_rows, n_lat_rows, seq, ctx_len, ctx_row,
              final):
    rows, d = xs.shape
    n_lat_tiles = n_lat_rows // TM
    row = functools.partial(_mod_row, n_lat_tiles=n_lat_tiles, tiles_per_seq=seq // TM, ctx_row=ctx_row)
    tb = TM // SUBLANES
    last = rows // SUBLANES - 1
    return pl.pallas_call(
        functools.partial(_ffn_kernel, n_lat_tiles=n_lat_tiles, seq=seq, ctx_len=ctx_len, final=final),
        grid=(out_rows // TM,),
        in_specs=[pl.BlockSpec((TM, d), lambda i: (i, 0)),
                  pl.BlockSpec((SUBLANES, d), lambda i: (jnp.maximum(i * tb - 1, 0), 0)),
                  pl.BlockSpec((SUBLANES, d), lambda i: (jnp.minimum((i + 1) * tb, last), 0)),
                  pl.BlockSpec((1, 6, d), lambda i: (row(i), 0, 0)),
                  _resident(w_up.shape), _resident(conv_w.shape), _resident((1, conv_b.shape[0])),
                  _resident(w_down.shape), _resident((1, d))],
        out_specs=pl.BlockSpec((TM, d), lambda i: (i, 0)),
        out_shape=jax.ShapeDtypeStruct((out_rows, d), F32),
        compiler_params=pltpu.CompilerParams(dimension_semantics=("parallel",),
                                             vmem_limit_bytes=VMEM_LIMIT),
        name="conv_ffn_final" if final else "conv_ffn",
    )(xs, xs, xs, mods, w_up, conv_w, conv_b.reshape(1, -1), w_down, final_norm.reshape(1, d))


def _inproj_odd_kernel(x_ref, mod_ref, w_ref, q_ref, v_ref, zf_ref, zb_ref, g_ref):
    m = mod_ref[0]
    h = _modnorm(x_ref[...], m[0:1], m[1:2]).astype(BF16)
    w = q_ref.shape[1]
    step = 512
    for c0 in range(0, w, step):
        sl = slice(c0, c0 + step)
        p = _dot(h, w_ref[:, c0:c0 + step])
        q_ref[:, sl] = (p * _sigmoid(p)).astype(BF16)
        v_ref[:, sl] = _dot(h, w_ref[:, w + c0:w + c0 + step]).astype(BF16)
        zf_ref[:, sl] = _dot(h, w_ref[:, 2 * w + c0:2 * w + c0 + step])
        zb_ref[:, sl] = _dot(h, w_ref[:, 3 * w + c0:3 * w + c0 + step])
        g_ref[:, sl] = _dot(h, w_ref[:, 4 * w + c0:4 * w + c0 + step]).astype(BF16)


def _inproj_odd(xs, mods, w_in, n_lat_rows, seq, ctx_row):
    rows, d = xs.shape
    w = w_in.shape[1] // 5
    row = functools.partial(_mod_row, n_lat_tiles=n_lat_rows // TM, tiles_per_seq=seq // TM, ctx_row=ctx_row)
    tok = lambda dt: jax.ShapeDtypeStruct((rows, w), dt)
    return pl.pallas_call(
        _inproj_odd_kernel,
        grid=(rows // TM,),
        in_specs=[pl.BlockSpec((TM, d), lambda i: (i, 0)),
                  pl.BlockSpec((1, 6, d), lambda i: (row(i), 0, 0)),
                  _resident(w_in.shape)],
        out_specs=[pl.BlockSpec((TM, w), lambda i: (i, 0))] * 5,
        out_shape=[tok(BF16), tok(BF16), tok(F32), tok(F32), tok(BF16)],
        compiler_params=pltpu.CompilerParams(dimension_semantics=("parallel",),
                                             vmem_limit_bytes=VMEM_LIMIT),
        name="odd_inproj",
    )(xs, mods, w_in)


def _cumsum_rows(tri, x):
    hi = x.astype(BF16)
    r1 = x - hi.astype(F32)
    mid = r1.astype(BF16)
    lo = (r1 - mid.astype(F32)).astype(BF16)
    return _dot(tri, hi) + _dot(tri, mid) + _dot(tri, lo)


def _hgrn2_kernel(lbl_ref, qc_ref, vc_ref, zfc_ref, zbc_ref, ql_ref, vl_ref, zfl_ref, zbl_ref,
                  o_ref, st_ref, *, layer):
    c = HG_CHUNK
    w = o_ref.shape[1]
    heads = w // HG_HEAD_DIM
    n_ctx = qc_ref.shape[0] // c
    n_lat = ql_ref.shape[0] // c
    r_i = lax.broadcasted_iota(jnp.int32, (c, c), 0)
    c_i = lax.broadcasted_iota(jnp.int32, (c, c), 1)

    def lower_bound(direction):
        lg = lbl_ref[direction]
        e = jnp.exp(lg - jnp.max(lg, axis=0, keepdims=True))
        p = e / jnp.sum(e, axis=0, keepdims=True)
        return jnp.sum(p[1:layer + 1], axis=0, keepdims=True)

    def chunk(q, v, z, lb, causal, tri, ref_row, end_row, want_out):
        f = lb + (1.0 - lb) * _sigmoid(z)
        k = 1.0 - f
        b = _cumsum_rows(tri, jnp.log(f))
        r = b[ref_row:ref_row + 1]
        be = b[end_row:end_row + 1]
        qf = q.astype(F32)
        kh = (k * jnp.exp(be - b)).astype(BF16)
        decay = jnp.exp(be)
        if want_out:
            qt = (qf * jnp.exp(b - r)).astype(BF16)
            kt = (k * jnp.exp(r - b)).astype(BF16)
            qh = (qf * jnp.exp(b)).astype(BF16)
        outs = []
        for h in range(heads):
            sl = slice(h * HG_HEAD_DIM, (h + 1) * HG_HEAD_DIM)
            st = st_ref[h]
            if want_out:
                a = jnp.where(causal, _dot_nt(qt[:, sl], kt[:, sl]), 0.0).astype(BF16)
                outs.append(_dot(a, v[:, sl]) + _dot_nt(qh[:, sl], st.astype(BF16)))
            st_ref[h] = st * decay[:, sl] + _dot_tn(v[:, sl], kh[:, sl])
        return jnp.concatenate(outs, axis=1) if want_out else None

    def run(direction, z_ctx_ref, z_lat_ref):
        fwd = direction == 0
        lb = lower_bound(direction)
        causal = (c_i <= r_i) if fwd else (c_i >= r_i)
        tri = jnp.where(causal, 1.0, 0.0).astype(BF16)
        ref_row = c // 2 - 1 if fwd else c // 2
        end_row = c - 1 if fwd else 0
        st_ref[...] = jnp.zeros_like(st_ref)

        def ctx_step(j, carry):
            rows = pl.ds(pl.multiple_of((j if fwd else n_ctx - 1 - j) * c, c), c)
            chunk(qc_ref[rows, :], vc_ref[rows, :], z_ctx_ref[rows, :], lb, causal, tri, ref_row, end_row, False)
            return carry

        def lat_step(j, carry):
            rows = pl.ds(pl.multiple_of((j if fwd else n_lat - 1 - j) * c, c), c)
            o = chunk(ql_ref[rows, :], vl_ref[rows, :], z_lat_ref[rows, :], lb, causal, tri, ref_row, end_row, True)
            if fwd:
                o_ref[rows, :] = o
            else:
                o_ref[rows, :] += o
            return carry

        lax.fori_loop(0, n_ctx, ctx_step, 0)
        lax.fori_loop(0, n_lat, lat_step, 0)

    run(0, zfc_ref, zfl_ref)
    run(1, zbc_ref, zbl_ref)


def _hgrn2(q, v, zf, zb, lb_logits, layer, batch, seq, ctx_len):
    rows, w = q.shape
    n_lat_rows = batch * seq
    ctx0 = n_lat_rows // ctx_len
    ctx = pl.BlockSpec((ctx_len, HG_LANES), lambda b, g: (ctx0 + b, g))
    lat = pl.BlockSpec((seq, HG_LANES), lambda b, g: (b, g))
    heads = HG_LANES // HG_HEAD_DIM
    return pl.pallas_call(
        functools.partial(_hgrn2_kernel, layer=layer),
        grid=(batch, w // HG_LANES),
        in_specs=[pl.BlockSpec((2, lb_logits.shape[1], HG_LANES), lambda b, g: (0, 0, g)),
                  ctx, ctx, ctx, ctx, lat, lat, lat, lat],
        out_specs=pl.BlockSpec((seq, HG_LANES), lambda b, g: (b, g)),
        out_shape=jax.ShapeDtypeStruct((n_lat_rows, w), F32),
        scratch_shapes=[pltpu.VMEM((heads, HG_HEAD_DIM, HG_HEAD_DIM), F32)],
        compiler_params=pltpu.CompilerParams(dimension_semantics=("parallel", "parallel"),
                                             vmem_limit_bytes=VMEM_LIMIT),
        name="hgrn2_scan",
    )(lb_logits, q, v, zf, zb, q, v, zf, zb)


def _outproj_odd_kernel(x_ref, mod_ref, o_ref, g_ref, gain_ref, w_ref, out_ref, hn_ref):
    for h in range(o_ref.shape[1] // HG_HEAD_DIM):
        sl = slice(h * HG_HEAD_DIM, (h + 1) * HG_HEAD_DIM)
        o = o_ref[:, sl]
        g = g_ref[:, sl].astype(F32)
        ms = jnp.mean(o * o, axis=-1, keepdims=True)
        hn_ref[:, sl] = (o * lax.rsqrt(ms + EPS) * gain_ref[:, sl] * (g * _sigmoid(g))).astype(BF16)
    out_ref[...] = x_ref[...] + mod_ref[0][2:3] * _dot(hn_ref[...], w_ref[...])


def _outproj_odd(xs, mods, o, g, gain, w_out, seq):
    out_rows, w = o.shape
    d = xs.shape[1]
    tps = seq // TM
    return pl.pallas_call(
        _outproj_odd_kernel,
        grid=(out_rows // TM,),
        in_specs=[pl.BlockSpec((TM, d), lambda i: (i, 0)),
                  pl.BlockSpec((1, 6, d), lambda i: (i // tps, 0, 0)),
                  pl.BlockSpec((TM, w), lambda i: (i, 0)),
                  pl.BlockSpec((TM, w), lambda i: (i, 0)),
                  _resident((1, w)), _resident(w_out.shape)],
        out_specs=pl.BlockSpec((TM, d), lambda i: (i, 0)),
        out_shape=jax.ShapeDtypeStruct((out_rows, d), F32),
        scratch_shapes=[pltpu.VMEM((TM, w), BF16)],
        compiler_params=pltpu.CompilerParams(dimension_semantics=("parallel",),
                                             vmem_limit_bytes=VMEM_LIMIT),
        name="odd_outproj",
    )(xs, mods, o, g, gain.reshape(1, w), w_out)


def kernel(x, c, ctx, c_ctx, w_mod, b_mod, ev_w_in, ev_w_out, da_lq1, da_lk1, da_lq2, da_lk2, da_subln, pool_w,
           pool_scale, hg_w_in, hg_w_out, hg_lb_logits, hg_norm, ffn_w_up, ffn_conv_w, ffn_conv_b, ffn_w_down,
           final_norm):
    batch, seq, d = x.shape
    ctx_len = ctx.shape[1]
    depth = w_mod.shape[0]
    n_lat_rows = batch * seq
    rows = n_lat_rows + batch * ctx_len
    assert ctx_len % TM == 0 and seq % TM == 0 and seq % TQ == 0

    xs = jnp.concatenate([x.reshape(n_lat_rows, d), ctx.reshape(batch * ctx_len, d)], axis=0)
    cond_rows = 2 * SUBLANES
    cond = jnp.zeros((cond_rows, d), F32).at[:batch].set(c).at[batch].set(c_ctx)
    mods_all = _modulation(cond, w_mod, b_mod).reshape(depth, cond_rows, 6, d)
    rope_tabs = _rope_tables(seq)

    for i in range(depth):
        last = i == depth - 1
        mods = mods_all[i]
        j = i // 2
        if i % 2 == 0:
            lambda_init = 0.8 - 0.6 * math.exp(-0.3 * i)
            q, k, v, u = _inproj_even(xs, mods, ev_w_in[j].astype(BF16), rope_tabs, n_lat_rows, seq, batch)
            lqk = jnp.stack([da_lq1[j], da_lk1[j], da_lq2[j], da_lk2[j]]).astype(F32)
            o = _diff_attention(q, k, v, lqk, da_subln[j], lambda_init, batch, seq, ctx_len)
            p = _pool(u, pool_w[j].astype(BF16), pool_scale[j], batch, seq, ctx_len)
            if last:
                xs, o, p = xs[:n_lat_rows], o[:n_lat_rows], p[:n_lat_rows]
            xs = _outproj_even(xs, mods, o, p, ev_w_out[j].astype(BF16), n_lat_rows, seq, batch)
        else:
            q, v, zf, zb, g = _inproj_odd(xs, mods, hg_w_in[j].astype(BF16), n_lat_rows, seq, batch)
            o = _hgrn2(q, v, zf, zb, hg_lb_logits.astype(F32), i, batch, seq, ctx_len)
            if not last:
                raise NotImplementedError("context readout of an HGRN2 layer that is not the last layer")
            xs = _outproj_odd(xs, mods, o, g, hg_norm[j], hg_w_out[j].astype(BF16), seq)
        out_rows = n_lat_rows if last else rows
        xs = _conv_ffn(xs, mods, ffn_w_up[i].astype(BF16), ffn_conv_w[i], ffn_conv_b[i],
                       ffn_w_down[i].astype(BF16), final_norm, out_rows, n_lat_rows, seq, ctx_len, batch, last)
    return xs.reshape(batch, seq, d)
```

```python
import functools
import math

import jax
import jax.numpy as jnp
from jax import lax
from jax.experimental import pallas as pl
from jax.experimental.pallas import tpu as pltpu

F32 = jnp.float32
BF16 = jnp.bfloat16

EPS = 1e-6
GRID_W = 64
DA_HEADS = 4
DA_QK_DIM = 64
ROPE_THETA = 10000.0
POOL_WINDOWS = (2, 4, 8, 16)
HG_HEAD_DIM = 128
LANES = 128
SUBLANES = 8
VMEM_LIMIT = 56 * 1024 * 1024

TM = 256
TQ = 256
HG_CHUNK = 32
HG_LANES = 256
HG_BLOCK = 256
FF_CHUNK = 256


def _resident(shape):
    nd = len(shape)
    return pl.BlockSpec(shape, lambda *_: (0,) * nd, pipeline_mode=pl.Buffered(1))


def _sigmoid(x):
    return 1.0 / (1.0 + jnp.exp(-x))


def _modnorm(x, shift, scale):
    ms = jnp.mean(x * x, axis=-1, keepdims=True)
    return (x * lax.rsqrt(ms + EPS)) * (1.0 + scale) + shift


def _dot(a, b):
    return jnp.dot(a, b, preferred_element_type=F32)


def _dot_nt(a, b):
    return lax.dot_general(a, b, (((1,), (1,)), ((), ())), preferred_element_type=F32)


def _dot_tn(a, b):
    return lax.dot_general(a, b, (((0,), (0,)), ((), ())), preferred_element_type=F32)


def _mod_kernel(cond_ref, w_ref, b_ref, out_ref):
    c = cond_ref[...]
    s = c * _sigmoid(c)
    out_ref[0] = jnp.dot(s, w_ref[0], precision=lax.Precision.HIGHEST,
                         preferred_element_type=F32) + b_ref[0]


def _modulation(cond, w_mod, b_mod):
    depth, d, n = w_mod.shape
    rows = cond.shape[0]
    tn = 1024
    return pl.pallas_call(
        _mod_kernel,
        grid=(depth, n // tn),
        in_specs=[pl.BlockSpec((rows, d), lambda l, j: (0, 0)),
                  pl.BlockSpec((1, d, tn), lambda l, j: (l, 0, j)),
                  pl.BlockSpec((1, 1, tn), lambda l, j: (l, 0, j))],
        out_specs=pl.BlockSpec((1, rows, tn), lambda l, j: (l, 0, j)),
        out_shape=jax.ShapeDtypeStruct((depth, rows, n), F32),
        compiler_params=pltpu.CompilerParams(dimension_semantics=("parallel", "parallel"),
                                             vmem_limit_bytes=VMEM_LIMIT),
        name="adaln_mod",
    )(cond, w_mod, b_mod.reshape(depth, 1, n))


def _mod_row(i, n_lat_tiles, tiles_per_seq, ctx_row):
    return jnp.where(i < n_lat_tiles, i // tiles_per_seq, ctx_row)


def _inproj_even_kernel(x_ref, mod_ref, w_ref, cos_ref, sa_ref, sb_ref, q_ref, k_ref, v_ref, u_ref):
    m = mod_ref[0]
    h = _modnorm(x_ref[...], m[0:1], m[1:2]).astype(BF16)
    w = q_ref.shape[1]
    reps = w // LANES
    cos = jnp.concatenate([cos_ref[...]] * reps, axis=1)
    sa = jnp.concatenate([sa_ref[...]] * reps, axis=1)
    sb = jnp.concatenate([sb_ref[...]] * reps, axis=1)
    half = DA_QK_DIM // 4

    def rope(p):
        return p * cos + pltpu.roll(p, w - half, axis=1) * sa + pltpu.roll(p, half, axis=1) * sb

    q_ref[...] = (rope(_dot(h, w_ref[:, 0:w])) * (DA_QK_DIM ** -0.5)).astype(BF16)
    k_ref[...] = rope(_dot(h, w_ref[:, w:2 * w])).astype(BF16)
    v_ref[...] = _dot(h, w_ref[:, 2 * w:3 * w]).astype(BF16)
    u_ref[...] = _dot(h, w_ref[:, 3 * w:4 * w])


def _inproj_even(xs, mods, w_in, rope_tabs, n_lat_rows, seq, ctx_row):
    rows, d = xs.shape
    w = w_in.shape[1] // 4
    n_lat_tiles = n_lat_rows // TM
    tps = seq // TM
    row = functools.partial(_mod_row, n_lat_tiles=n_lat_tiles, tiles_per_seq=tps, ctx_row=ctx_row)
    tab_spec = pl.BlockSpec((TM, LANES), lambda i: (jnp.where(i < n_lat_tiles, i % tps, tps), 0))
    tok = lambda dt: jax.ShapeDtypeStruct((rows, w), dt)
    return pl.pallas_call(
        _inproj_even_kernel,
        grid=(rows // TM,),
        in_specs=[pl.BlockSpec((TM, d), lambda i: (i, 0)),
                  pl.BlockSpec((1, 6, d), lambda i: (row(i), 0, 0)),
                  _resident(w_in.shape), tab_spec, tab_spec, tab_spec],
        out_specs=[pl.BlockSpec((TM, w), lambda i: (i, 0))] * 4,
        out_shape=[tok(BF16), tok(BF16), tok(BF16), tok(F32)],
        compiler_params=pltpu.CompilerParams(dimension_semantics=("parallel",),
                                             vmem_limit_bytes=VMEM_LIMIT),
        name="even_inproj",
    )(xs, mods, w_in, *rope_tabs)


def _rope_tables(seq):
    rows_n = seq // GRID_W
    row = jnp.repeat(jnp.arange(rows_n, dtype=F32), GRID_W)
    col = jnp.tile(jnp.arange(GRID_W, dtype=F32), rows_n)
    n_freq = DA_QK_DIM // 4
    inv = ROPE_THETA ** (-jnp.arange(n_freq, dtype=F32) / n_freq)
    ang_r, ang_c = row[:, None] * inv, col[:, None] * inv
    zeros = jnp.zeros_like(ang_r)
    cos64 = jnp.concatenate([jnp.cos(ang_r)] * 2 + [jnp.cos(ang_c)] * 2, axis=1)
    sa64 = jnp.concatenate([-jnp.sin(ang_r), zeros, -jnp.sin(ang_c), zeros], axis=1)
    sb64 = jnp.concatenate([zeros, jnp.sin(ang_r), zeros, jnp.sin(ang_c)], axis=1)

    def finish(t, fill):
        t = jnp.concatenate([t, t], axis=1)
        return jnp.concatenate([t, jnp.full((TM, LANES), fill, F32)], axis=0)

    return finish(cos64, 1.0), finish(sa64, 0.0), finish(sb64, 0.0)


def _attn_kernel(*refs, lambda_init, has_lat):
    if has_lat:
        lqk_ref, gain_ref, q_ref, kc_ref, vc_ref, kl_ref, vl_ref, o_ref = refs
    else:
        lqk_ref, gain_ref, q_ref, kc_ref, vc_ref, o_ref = refs
    a = lqk_ref[...]
    lam = (jnp.exp(jnp.sum(a[0:1] * a[1:2], axis=-1, keepdims=True))
           - jnp.exp(jnp.sum(a[2:3] * a[3:4], axis=-1, keepdims=True)) + lambda_init)
    q = q_ref[...]
    lane = lax.broadcasted_iota(jnp.int32, (1, LANES), 1)
    zero = jnp.zeros_like(q)

    def softmax_pv(qm):
        sc = _dot_nt(qm, kc_ref[...])
        m = jnp.max(sc, axis=-1, keepdims=True)
        if has_lat:
            sl = _dot_nt(qm, kl_ref[...])
            m = jnp.maximum(m, jnp.max(sl, axis=-1, keepdims=True))
        ec = jnp.exp(sc - m)
        l = jnp.sum(ec, axis=-1, keepdims=True)
        o = _dot(ec.astype(BF16), vc_ref[...])
        if has_lat:
            el = jnp.exp(sl - m)
            l = l + jnp.sum(el, axis=-1, keepdims=True)
            o = o + _dot(el.astype(BF16), vl_ref[...])
        return o / l

    o = softmax_pv(jnp.where(lane < DA_QK_DIM, q, zero)) - lam * softmax_pv(jnp.where(lane >= DA_QK_DIM, q, zero))
    ms = jnp.mean(o * o, axis=-1, keepdims=True)
    o_ref[...] = (o * lax.rsqrt(ms + EPS) * (gain_ref[...] * (1.0 - lambda_init))).astype(BF16)


def _diff_attention(q, k, v, lqk, subln, lambda_init, batch, seq, ctx_len):
    rows, w = q.shape
    n_lat_rows = batch * seq
    ctx0 = n_lat_rows // ctx_len
    gain = subln.reshape(1, w)
    small = [pl.BlockSpec(lqk.shape, lambda *_: (0, 0)),
             pl.BlockSpec((1, LANES), lambda b, h, *_: (0, h))]
    ctx_kv = pl.BlockSpec((ctx_len, LANES), lambda b, h, *_: (ctx0 + b, h))
    lat_kv = pl.BlockSpec((seq, LANES), lambda b, h, *_: (b, h))
    nq = seq // TQ
    o_lat = pl.pallas_call(
        functools.partial(_attn_kernel, lambda_init=lambda_init, has_lat=True),
        grid=(batch, DA_HEADS, nq),
        in_specs=small + [pl.BlockSpec((TQ, LANES), lambda b, h, i: (b * nq + i, h)),
                          ctx_kv, ctx_kv, lat_kv, lat_kv],
        out_specs=pl.BlockSpec((TQ, LANES), lambda b, h, i: (b * nq + i, h)),
        out_shape=jax.ShapeDtypeStruct((n_lat_rows, w), BF16),
        compiler_params=pltpu.CompilerParams(dimension_semantics=("parallel", "parallel", "parallel"),
                                             vmem_limit_bytes=VMEM_LIMIT),
        name="diff_attn_latent",
    )(lqk, gain, q, k, v, k, v)
    o_ctx = pl.pallas_call(
        functools.partial(_attn_kernel, lambda_init=lambda_init, has_lat=False),
        grid=(batch, DA_HEADS),
        in_specs=small + [ctx_kv, ctx_kv, ctx_kv],
        out_specs=pl.BlockSpec((ctx_len, LANES), lambda b, h: (b, h)),
        out_shape=jax.ShapeDtypeStruct((rows - n_lat_rows, w), BF16),
        compiler_params=pltpu.CompilerParams(dimension_semantics=("parallel", "parallel"),
                                             vmem_limit_bytes=VMEM_LIMIT),
        name="diff_attn_context",
    )(lqk, gain, q, k, v)
    return jnp.concatenate([o_lat, o_ctx], axis=0)


def _pool_kernel(u_ref, pw_ref, ps_ref, out_ref):
    n = u_ref.shape[0]
    t = lax.broadcasted_iota(jnp.int32, (n, 1), 0)
    for g, w in enumerate(POOL_WINDOWS):
        sl = slice(g * LANES, (g + 1) * LANES)
        ug = u_ref[:, sl]
        lo, hi = -(w // 2), w - w // 2
        acc = ug
        for off in range(lo, hi):
            if off == 0:
                continue
            valid = (t + off >= 0) & (t + off < n)
            acc = acc + jnp.where(valid, pltpu.roll(ug, (-off) % n, axis=0), 0.0)
        cnt = (jnp.minimum(t + hi, n) - jnp.maximum(t + lo, 0)).astype(F32)
        dlt = acc / cnt - ug
        y = _dot(dlt.astype(BF16), pw_ref[g]) * ps_ref[:, sl]
        out_ref[:, sl] = y.astype(BF16)


def _pool(u, pool_w, pool_scale, batch, seq, ctx_len):
    rows, w = u.shape
    n_lat_rows = batch * seq
    ctx0 = n_lat_rows // ctx_len

    def call(n, off, name):
        return pl.pallas_call(
            _pool_kernel,
            grid=(batch,),
            in_specs=[pl.BlockSpec((n, w), lambda b: (off + b, 0)),
                      pl.BlockSpec(pool_w.shape, lambda b: (0, 0, 0)),
                      pl.BlockSpec((1, w), lambda b: (0, 0))],
            out_specs=pl.BlockSpec((n, w), lambda b: (b, 0)),
            out_shape=jax.ShapeDtypeStruct((batch * n, w), BF16),
            compiler_params=pltpu.CompilerParams(dimension_semantics=("parallel",),
                                                 vmem_limit_bytes=VMEM_LIMIT),
            name=name,
        )(u, pool_w, pool_scale.reshape(1, w))

    return jnp.concatenate([call(seq, 0, "pool_latent"), call(ctx_len, ctx0, "pool_context")], axis=0)


def _outproj_even_kernel(x_ref, mod_ref, o_ref, p_ref, w_ref, out_ref):
    k = o_ref.shape[1]
    y = _dot(o_ref[...], w_ref[0:k, :]) + _dot(p_ref[...], w_ref[k:, :])
    out_ref[...] = x_ref[...] + mod_ref[0][2:3] * y


def _outproj_even(xs, mods, o, p, w_out, n_lat_rows, seq, ctx_row):
    rows, d = xs.shape
    k = o.shape[1]
    row = functools.partial(_mod_row, n_lat_tiles=n_lat_rows // TM, tiles_per_seq=seq // TM, ctx_row=ctx_row)
    return pl.pallas_call(
        _outproj_even_kernel,
        grid=(rows // TM,),
        in_specs=[pl.BlockSpec((TM, d), lambda i: (i, 0)),
                  pl.BlockSpec((1, 6, d), lambda i: (row(i), 0, 0)),
                  pl.BlockSpec((TM, k), lambda i: (i, 0)),
                  pl.BlockSpec((TM, k), lambda i: (i, 0)),
                  _resident(w_out.shape)],
        out_specs=pl.BlockSpec((TM, d), lambda i: (i, 0)),
        out_shape=jax.ShapeDtypeStruct((rows, d), F32),
        compiler_params=pltpu.CompilerParams(dimension_semantics=("parallel",),
                                             vmem_limit_bytes=VMEM_LIMIT),
        name="even_outproj",
    )(xs, mods, o, p, w_out)


def _ffn_kernel(x_ref, xp_ref, xn_ref, mod_ref, wu_ref, cw_ref, cb_ref, wd_ref, fin_ref, out_ref,
                *, n_lat_tiles, seq, ctx_len, final):
    i = pl.program_id(0)
    tm = x_ref.shape[0]
    d_ff = wd_ref.shape[0]
    length = jnp.where(i < n_lat_tiles, seq, ctx_len)
    at_start = (i * tm) % length == 0
    at_end = ((i + 1) * tm) % length == 0
    m = mod_ref[0]
    shift, scale = m[3:4], m[4:5]
    x = x_ref[...]
    hp = jnp.where(at_start, 0.0, _modnorm(xp_ref[...], shift, scale))
    hn = jnp.where(at_end, 0.0, _modnorm(xn_ref[...], shift, scale))
    h = jnp.concatenate([hp, _modnorm(x, shift, scale), hn], axis=0).astype(BF16)
    ext = tm + 2 * SUBLANES

    def conv(u, c0):
        cw = cw_ref[:, c0:c0 + FF_CHUNK]
        y = (pltpu.roll(u, 1, axis=0) * cw[0:1] + u * cw[1:2] + pltpu.roll(u, ext - 1, axis=0) * cw[2:3]
             + cb_ref[:, c0:c0 + FF_CHUNK])
        return y[SUBLANES:SUBLANES + tm]

    acc = jnp.zeros((tm, out_ref.shape[1]), F32)
    for j in range(d_ff // FF_CHUNK):
        ca, cg = j * FF_CHUNK, d_ff + j * FF_CHUNK
        a = conv(_dot(h, wu_ref[:, ca:ca + FF_CHUNK]), ca)
        g = conv(_dot(h, wu_ref[:, cg:cg + FF_CHUNK]), cg)
        act = (a * (g * _sigmoid(g))).astype(BF16)
        acc = acc + _dot(act, wd_ref[ca:ca + FF_CHUNK, :])
    y = x + m[5:6] * acc
    if final:
        ms = jnp.mean(y * y, axis=-1, keepdims=True)
        y = y * lax.rsqrt(ms + EPS) * fin_ref[...]
    out_ref[...] = y


def _conv_ffn(xs, mods, w_up, conv_w, conv_b, w_down, final_norm, out_rows, n_lat_rows, seq, ctx_len, ctx_row,
              final):
    rows, d = xs.shape
    n_lat_tiles = n_lat_rows // TM
    row = functools.partial(_mod_row, n_lat_tiles=n_lat_tiles, tiles_per_seq=seq // TM, ctx_row=ctx_row)
    tb = TM // SUBLANES
    last = rows // SUBLANES - 1
    return pl.pallas_call(
        functools.partial(_ffn_kernel, n_lat_tiles=n_lat_tiles, seq=seq, ctx_len=ctx_len, final=final),
        grid=(out_rows // TM,),
        in_specs=[pl.BlockSpec((TM, d), lambda i: (i, 0)),
                  pl.BlockSpec((SUBLANES, d), lambda i: (jnp.maximum(i * tb - 1, 0), 0)),
                  pl.BlockSpec((SUBLANES, d), lambda i: (jnp.minimum((i + 1) * tb, last), 0)),
                  pl.BlockSpec((1, 6, d), lambda i: (row(i), 0, 0)),
                  _resident(w_up.shape), _resident(conv_w.shape), _resident((1, conv_b.shape[0])),
                  _resident(w_down.shape), _resident((1, d))],
        out_specs=pl.BlockSpec((TM, d), lambda i: (i, 0)),
        out_shape=jax.ShapeDtypeStruct((out_rows, d), F32),
        compiler_params=pltpu.CompilerParams(dimension_semantics=("parallel",),
                                             vmem_limit_bytes=VMEM_LIMIT),
        name="conv_ffn_final" if final else "conv_ffn",
    )(xs, xs, xs, mods, w_up, conv_w, conv_b.reshape(1, -1), w_down, final_norm.reshape(1, d))


def _inproj_odd_kernel(x_ref, mod_ref, w_ref, q_ref, v_ref, zf_ref, zb_ref, g_ref):
    m = mod_ref[0]
    h = _modnorm(x_ref[...], m[0:1], m[1:2]).astype(BF16)
    w = q_ref.shape[1]
    step = 512
    for c0 in range(0, w, step):
        sl = slice(c0, c0 + step)
        p = _dot(h, w_ref[:, c0:c0 + step])
        q_ref[:, sl] = (p * _sigmoid(p)).astype(BF16)
        v_ref[:, sl] = _dot(h, w_ref[:, w + c0:w + c0 + step]).astype(BF16)
        zf_ref[:, sl] = _dot(h, w_ref[:, 2 * w + c0:2 * w + c0 + step])
        zb_ref[:, sl] = _dot(h, w_ref[:, 3 * w + c0:3 * w + c0 + step])
        g_ref[:, sl] = _dot(h, w_ref[:, 4 * w + c0:4 * w + c0 + step]).astype(BF16)


def _inproj_odd(xs, mods, w_in, n_lat_rows, seq, ctx_row):
    rows, d = xs.shape
    w = w_in.shape[1] // 5
    row = functools.partial(_mod_row, n_lat_tiles=n_lat_rows // TM, tiles_per_seq=seq // TM, ctx_row=ctx_row)
    tok = lambda dt: jax.ShapeDtypeStruct((rows, w), dt)
    return pl.pallas_call(
        _inproj_odd_kernel,
        grid=(rows // TM,),
        in_specs=[pl.BlockSpec((TM, d), lambda i: (i, 0)),
                  pl.BlockSpec((1, 6, d), lambda i: (row(i), 0, 0)),
                  _resident(w_in.shape)],
        out_specs=[pl.BlockSpec((TM, w), lambda i: (i, 0))] * 5,
        out_shape=[tok(BF16), tok(BF16), tok(F32), tok(F32), tok(BF16)],
        compiler_params=pltpu.CompilerParams(dimension_semantics=("parallel",),
                                             vmem_limit_bytes=VMEM_LIMIT),
        name="odd_inproj",
    )(xs, mods, w_in)


def _cumsum_rows(tri, x):
    w = x.shape[1]
    hi = x.astype(BF16)
    r1 = x - hi.astype(F32)
    mid = r1.astype(BF16)
    lo = (r1 - mid.astype(F32)).astype(BF16)
    s = _dot(tri, jnp.concatenate([hi, mid, lo], axis=1))
    return s[:, 0:w] + s[:, w:2 * w] + s[:, 2 * w:3 * w]


def _hgrn2_kernel(lbl_ref, qc_ref, vc_ref, zfc_ref, zbc_ref, ql_ref, vl_ref, zfl_ref, zbl_ref,
                  o_ref, ob_ref, st_ref, ds_ref, *, layer):
    c = HG_CHUNK
    rb = HG_BLOCK
    n_sub = rb // c
    w = o_ref.shape[1]
    heads = w // HG_HEAD_DIM
    n_ctx = qc_ref.shape[0] // rb
    n_lat = ql_ref.shape[0] // rb
    r_i = lax.broadcasted_iota(jnp.int32, (rb, rb), 0)
    c_i = lax.broadcasted_iota(jnp.int32, (rb, rb), 1)
    same_chunk = (r_i // c) == (c_i // c)

    def direction_constants(d):
        fwd = d == 0
        lg = lbl_ref[d]
        e = jnp.exp(lg - jnp.max(lg, axis=0, keepdims=True))
        p = e / jnp.sum(e, axis=0, keepdims=True)
        lb = jnp.sum(p[1:layer + 1], axis=0, keepdims=True)
        causal = same_chunk & ((c_i <= r_i) if fwd else (c_i >= r_i))
        tri = jnp.where(causal, 1.0, 0.0).astype(BF16)
        ref_row = c // 2 - 1 if fwd else c // 2
        end_row = c - 1 if fwd else 0
        order = range(n_sub) if fwd else range(n_sub - 1, -1, -1)
        return lb, causal, tri, ref_row, end_row, order

    consts = [direction_constants(0), direction_constants(1)]

    def block(d, q, v, z, want_out, out):
        lb, causal, tri, ref_row, end_row, order = consts[d]
        split = lambda a: a.reshape(n_sub, c, w)
        merge = lambda a: a.reshape(rb, w).astype(BF16)
        hsl = [slice(h * HG_HEAD_DIM, (h + 1) * HG_HEAD_DIM) for h in range(heads)]
        csl = [slice(j * c, (j + 1) * c) for j in range(n_sub)]
        f = lb + (1.0 - lb) * _sigmoid(z)
        k = split(1.0 - f)
        logf = jnp.log(f)
        yield
        b = split(_cumsum_rows(tri, logf))
        yield
        r = b[:, ref_row:ref_row + 1, :]
        be = b[:, end_row:end_row + 1, :]
        kh = merge(k * jnp.exp(be - b))
        decay = jnp.exp(be)
        if want_out:
            qf = split(q.astype(F32))
            qt = merge(qf * jnp.exp(b - r))
            kt = merge(k * jnp.exp(r - b))
            qh = merge(qf * jnp.exp(b))
        yield
        for h in range(heads):
            for j in range(n_sub):
                ds_ref[d, h, j] = _dot_tn(v[csl[j], hsl[h]], kh[csl[j], hsl[h]])
        if want_out:
            a_raw = [_dot_nt(qt[:, sl], kt[:, sl]) for sl in hsl]
        yield
        states = []
        for h in range(heads):
            st = st_ref[d, h]
            before = [None] * n_sub
            for j in order:
                before[j] = st.astype(BF16)
                st = st * decay[j][:, hsl[h]] + ds_ref[d, h, j]
            st_ref[d, h] = st
            states.append(before)
        if want_out:
            a = [jnp.where(causal, x, 0.0).astype(BF16) for x in a_raw]
        yield
        if want_out:
            outs = []
            for h in range(heads):
                inter = [_dot_nt(qh[csl[j], hsl[h]], states[h][j]) for j in range(n_sub)]
                outs.append(_dot(a[h], v[:, hsl[h]]) + jnp.concatenate(inter, axis=0))
            out[d] = jnp.concatenate(outs, axis=1)
        yield

    def run_interleaved(gens):
        for _ in zip(*gens):
            pass

    def block_rows(d, j, n):
        return pl.ds(pl.multiple_of((j if d == 0 else n - 1 - j) * rb, rb), rb)

    def ctx_step(j, carry):
        gens = []
        for d, z_ref in ((0, zfc_ref), (1, zbc_ref)):
            rows = block_rows(d, j, n_ctx)
            gens.append(block(d, qc_ref[rows, :], vc_ref[rows, :], z_ref[rows, :], False, None))
        run_interleaved(gens)
        return carry

    def lat_step(j, carry):
        out = {}
        rows = [block_rows(d, j, n_lat) for d in (0, 1)]
        gens = [block(d, ql_ref[rows[d], :], vl_ref[rows[d], :], z_ref[rows[d], :], True, out)
                for d, z_ref in ((0, zfl_ref), (1, zbl_ref))]
        run_interleaved(gens)
        o_ref[rows[0], :] = out[0]
        ob_ref[rows[1], :] = out[1]
        return carry

    st_ref[...] = jnp.zeros_like(st_ref)
    lax.fori_loop(0, n_ctx, ctx_step, 0)
    lax.fori_loop(0, n_lat, lat_step, 0)
    o_ref[...] += ob_ref[...]


def _hgrn2(q, v, zf, zb, lb_logits, layer, batch, seq, ctx_len):
    rows, w = q.shape
    n_lat_rows = batch * seq
    ctx0 = n_lat_rows // ctx_len
    ctx = pl.BlockSpec((ctx_len, HG_LANES), lambda b, g: (ctx0 + b, g))
    lat = pl.BlockSpec((seq, HG_LANES), lambda b, g: (b, g))
    heads = HG_LANES // HG_HEAD_DIM
    return pl.pallas_call(
        functools.partial(_hgrn2_kernel, layer=layer),
        grid=(batch, w // HG_LANES),
        in_specs=[pl.BlockSpec((2, lb_logits.shape[1], HG_LANES), lambda b, g: (0, 0, g)),
                  ctx, ctx, ctx, ctx, lat, lat, lat, lat],
        out_specs=pl.BlockSpec((seq, HG_LANES), lambda b, g: (b, g)),
        out_shape=jax.ShapeDtypeStruct((n_lat_rows, w), F32),
        scratch_shapes=[pltpu.VMEM((seq, HG_LANES), F32),
                        pltpu.VMEM((2, heads, HG_HEAD_DIM, HG_HEAD_DIM), F32),
                        pltpu.VMEM((2, heads, HG_BLOCK // HG_CHUNK, HG_HEAD_DIM, HG_HEAD_DIM), F32)],
        compiler_params=pltpu.CompilerParams(dimension_semantics=("parallel", "parallel"),
                                             vmem_limit_bytes=VMEM_LIMIT),
        name="hgrn2_scan",
    )(lb_logits, q, v, zf, zb, q, v, zf, zb)


def _outproj_odd_kernel(x_ref, mod_ref, o_ref, g_ref, gain_ref, w_ref, out_ref, hn_ref):
    for h in range(o_ref.shape[1] // HG_HEAD_DIM):
        sl = slice(h * HG_HEAD_DIM, (h + 1) * HG_HEAD_DIM)
        o = o_ref[:, sl]
        g = g_ref[:, sl].astype(F32)
        ms = jnp.mean(o * o, axis=-1, keepdims=True)
        hn_ref[:, sl] = (o * lax.rsqrt(ms + EPS) * gain_ref[:, sl] * (g * _sigmoid(g))).astype(BF16)
    out_ref[...] = x_ref[...] + mod_ref[0][2:3] * _dot(hn_ref[...], w_ref[...])


def _outproj_odd(xs, mods, o, g, gain, w_out, seq):
    out_rows, w = o.shape
    d = xs.shape[1]
    tps = seq // TM
    return pl.pallas_call(
        _outproj_odd_kernel,
        grid=(out_rows // TM,),
        in_specs=[pl.BlockSpec((TM, d), lambda i: (i, 0)),
                  pl.BlockSpec((1, 6, d), lambda i: (i // tps, 0, 0)),
                  pl.BlockSpec((TM, w), lambda i: (i, 0)),
                  pl.BlockSpec((TM, w), lambda i: (i, 0)),
                  _resident((1, w)), _resident(w_out.shape)],
        out_specs=pl.BlockSpec((TM, d), lambda i: (i, 0)),
        out_shape=jax.ShapeDtypeStruct((out_rows, d), F32),
        scratch_shapes=[pltpu.VMEM((TM, w), BF16)],
        compiler_params=pltpu.CompilerParams(dimension_semantics=("parallel",),
                                             vmem_limit_bytes=VMEM_LIMIT),
        name="odd_outproj",
    )(xs, mods, o, g, gain.reshape(1, w), w_out)


def kernel(x, c, ctx, c_ctx, w_mod, b_mod, ev_w_in, ev_w_out, da_lq1, da_lk1, da_lq2, da_lk2, da_subln, pool_w,
           pool_scale, hg_w_in, hg_w_out, hg_lb_logits, hg_norm, ffn_w_up, ffn_conv_w, ffn_conv_b, ffn_w_down,
           final_norm):
    batch, seq, d = x.shape
    ctx_len = ctx.shape[1]
    depth = w_mod.shape[0]
    n_lat_rows = batch * seq
    rows = n_lat_rows + batch * ctx_len
    assert ctx_len % TM == 0 and seq % TM == 0 and seq % TQ == 0

    xs = jnp.concatenate([x.reshape(n_lat_rows, d), ctx.reshape(batch * ctx_len, d)], axis=0)
    cond_rows = 2 * SUBLANES
    cond = jnp.zeros((cond_rows, d), F32).at[:batch].set(c).at[batch].set(c_ctx)
    mods_all = _modulation(cond, w_mod, b_mod).reshape(depth, cond_rows, 6, d)
    rope_tabs = _rope_tables(seq)

    for i in range(depth):
        last = i == depth - 1
        mods = mods_all[i]
        j = i // 2
        if i % 2 == 0:
            lambda_init = 0.8 - 0.6 * math.exp(-0.3 * i)
            q, k, v, u = _inproj_even(xs, mods, ev_w_in[j].astype(BF16), rope_tabs, n_lat_rows, seq, batch)
            lqk = jnp.stack([da_lq1[j], da_lk1[j], da_lq2[j], da_lk2[j]]).astype(F32)
            o = _diff_attention(q, k, v, lqk, da_subln[j], lambda_init, batch, seq, ctx_len)
            p = _pool(u, pool_w[j].astype(BF16), pool_scale[j], batch, seq, ctx_len)
            if last:
                xs, o, p = xs[:n_lat_rows], o[:n_lat_rows], p[:n_lat_rows]
            xs = _outproj_even(xs, mods, o, p, ev_w_out[j].astype(BF16), n_lat_rows, seq, batch)
        else:
            q, v, zf, zb, g = _inproj_odd(xs, mods, hg_w_in[j].astype(BF16), n_lat_rows, seq, batch)
            o = _hgrn2(q, v, zf, zb, hg_lb_logits.astype(F32), i, batch, seq, ctx_len)
            if not last:
                raise NotImplementedError("context readout of an HGRN2 layer that is not the last layer")
            xs = _outproj_odd(xs, mods, o, g, hg_norm[j], hg_w_out[j].astype(BF16), seq)
        out_rows = n_lat_rows if last else rows
        xs = _conv_ffn(xs, mods, ffn_w_up[i].astype(BF16), ffn_conv_w[i], ffn_conv_b[i],
                       ffn_w_down[i].astype(BF16), final_norm, out_rows, n_lat_rows, seq, ctx_len, batch, last)
    return xs.reshape(batch, seq, d)
```

```python
import functools
import math

import jax
import jax.numpy as jnp
from jax import lax
from jax.experimental import pallas as pl
from jax.experimental.pallas import tpu as pltpu

F32 = jnp.float32
BF16 = jnp.bfloat16

EPS = 1e-6
GRID_W = 64
DA_HEADS = 4
DA_QK_DIM = 64
ROPE_THETA = 10000.0
POOL_WINDOWS = (2, 4, 8, 16)
HG_HEAD_DIM = 128
LANES = 128
SUBLANES = 8
VMEM_LIMIT = 56 * 1024 * 1024

TM_LAT = 512
TM_CTX = 256
TQ = 256
ATTN_KEY_BLOCK = 256
HG_CHUNK = 32
HG_LANES = 256
HG_BLOCK = 256
FF_CHUNK = 1408
LOG2E = math.log2(math.e)


def _resident(shape):
    nd = len(shape)
    return pl.BlockSpec(shape, lambda *_: (0,) * nd, pipeline_mode=pl.Buffered(1))


def _params(n_axes):
    return pltpu.CompilerParams(dimension_semantics=("parallel",) * n_axes, vmem_limit_bytes=VMEM_LIMIT)


def _mod_spec(d, tm, seq_len, fixed_row):
    if fixed_row is None:
        tiles_per_seq = seq_len // tm
        return pl.BlockSpec((1, 6, d), lambda i: (i // tiles_per_seq, 0, 0))
    return pl.BlockSpec((1, 6, d), lambda i: (fixed_row, 0, 0))


def _sigmoid(x):
    return 1.0 / (1.0 + jnp.exp(-x))


def _modnorm(x, shift, scale):
    ms = jnp.mean(x * x, axis=-1, keepdims=True)
    return (x * lax.rsqrt(ms + EPS)) * (1.0 + scale) + shift


def _dot(a, b):
    return jnp.dot(a, b, preferred_element_type=F32)


def _dot_nt(a, b):
    return lax.dot_general(a, b, (((1,), (1,)), ((), ())), preferred_element_type=F32)


def _dot_tn(a, b):
    return lax.dot_general(a, b, (((0,), (0,)), ((), ())), preferred_element_type=F32)


def _mod_kernel(cond_ref, w_ref, b_ref, out_ref):
    c = cond_ref[...]
    s = c * _sigmoid(c)
    out_ref[0] = jnp.dot(s, w_ref[0], precision=lax.Precision.HIGHEST,
                         preferred_element_type=F32) + b_ref[0]


def _modulation(cond, w_mod, b_mod):
    depth, d, n = w_mod.shape
    rows = cond.shape[0]
    tn = 1024
    return pl.pallas_call(
        _mod_kernel,
        grid=(depth, n // tn),
        in_specs=[pl.BlockSpec((rows, d), lambda l, j: (0, 0)),
                  pl.BlockSpec((1, d, tn), lambda l, j: (l, 0, j)),
                  pl.BlockSpec((1, 1, tn), lambda l, j: (l, 0, j))],
        out_specs=pl.BlockSpec((1, rows, tn), lambda l, j: (l, 0, j)),
        out_shape=jax.ShapeDtypeStruct((depth, rows, n), F32),
        compiler_params=_params(2),
        name="adaln_mod",
    )(cond, w_mod, b_mod.reshape(depth, 1, n))


def _inproj_even_kernel(*refs, has_rope):
    if has_rope:
        x_ref, mod_ref, w_ref, wvt_ref, cos_ref, sa_ref, sb_ref, q_ref, k_ref, vt_ref, u_ref = refs
    else:
        x_ref, mod_ref, w_ref, wvt_ref, q_ref, k_ref, vt_ref, u_ref = refs
    m = mod_ref[0]
    h = _modnorm(x_ref[...], m[0:1], m[1:2]).astype(BF16)
    w = q_ref.shape[1]
    pq = _dot(h, w_ref[:, 0:w])
    pk = _dot(h, w_ref[:, w:2 * w])
    if has_rope:
        reps = w // LANES
        cos = jnp.concatenate([cos_ref[...]] * reps, axis=1)
        sa = jnp.concatenate([sa_ref[...]] * reps, axis=1)
        sb = jnp.concatenate([sb_ref[...]] * reps, axis=1)
        half = DA_QK_DIM // 4

        def rope(p):
            return p * cos + pltpu.roll(p, w - half, axis=1) * sa + pltpu.roll(p, half, axis=1) * sb

        pq, pk = rope(pq), rope(pk)
    q_ref[...] = (pq * (DA_QK_DIM ** -0.5 * LOG2E)).astype(BF16)
    k_ref[...] = pk.astype(BF16)
    vt_ref[...] = _dot_nt(wvt_ref[...], h).astype(BF16)
    u_ref[...] = _dot(h, w_ref[:, 2 * w:3 * w])


def _inproj_even(xs, mods, w_qku, w_vt, rope_tabs, tm, seq_len, mod_row, name):
    rows, d = xs.shape
    w = w_vt.shape[0]
    has_rope = rope_tabs is not None
    in_specs = [pl.BlockSpec((tm, d), lambda i: (i, 0)), _mod_spec(d, tm, seq_len, mod_row),
                _resident(w_qku.shape), _resident(w_vt.shape)]
    args = [xs, mods, w_qku, w_vt]
    if has_rope:
        tiles_per_seq = seq_len // tm
        in_specs += [pl.BlockSpec((tm, LANES), lambda i: (i % tiles_per_seq, 0))] * 3
        args += list(rope_tabs)
    tok = lambda dt: jax.ShapeDtypeStruct((rows, w), dt)
    tok_spec = pl.BlockSpec((tm, w), lambda i: (i, 0))
    return pl.pallas_call(
        functools.partial(_inproj_even_kernel, has_rope=has_rope),
        grid=(rows // tm,),
        in_specs=in_specs,
        out_specs=[tok_spec, tok_spec, pl.BlockSpec((w, tm), lambda i: (0, i)), tok_spec],
        out_shape=[tok(BF16), tok(BF16), jax.ShapeDtypeStruct((w, rows), BF16), tok(F32)],
        compiler_params=_params(1),
        name=name,
    )(*args)


def _rope_tables(seq):
    rows_n = seq // GRID_W
    row = jnp.repeat(jnp.arange(rows_n, dtype=F32), GRID_W)
    col = jnp.tile(jnp.arange(GRID_W, dtype=F32), rows_n)
    n_freq = DA_QK_DIM // 4
    inv = ROPE_THETA ** (-jnp.arange(n_freq, dtype=F32) / n_freq)
    ang_r, ang_c = row[:, None] * inv, col[:, None] * inv
    zeros = jnp.zeros_like(ang_r)
    cos64 = jnp.concatenate([jnp.cos(ang_r)] * 2 + [jnp.cos(ang_c)] * 2, axis=1)
    sa64 = jnp.concatenate([-jnp.sin(ang_r), zeros, -jnp.sin(ang_c), zeros], axis=1)
    sb64 = jnp.concatenate([zeros, jnp.sin(ang_r), zeros, jnp.sin(ang_c)], axis=1)
    return tuple(jnp.concatenate([t, t], axis=1) for t in (cos64, sa64, sb64))


def _attn_kernel(*refs, lambda_init, has_lat):
    if has_lat:
        lqk_ref, gain_ref, q_ref, kc_ref, vtc_ref, kl_ref, vtl_ref, o_ref, e_ref = refs
    else:
        lqk_ref, gain_ref, q_ref, kc_ref, vtc_ref, o_ref, e_ref = refs
    a = lqk_ref[...]
    lam = (jnp.exp(jnp.sum(a[0:1] * a[1:2], axis=-1, keepdims=True))
           - jnp.exp(jnp.sum(a[2:3] * a[3:4], axis=-1, keepdims=True)) + lambda_init)
    q = q_ref[...]
    lane = lax.broadcasted_iota(jnp.int32, (1, LANES), 1)
    zero = jnp.zeros_like(q)
    n_ctx = kc_ref.shape[0]
    key_blocks = [(kc_ref, 0, n_ctx)]
    if has_lat:
        key_blocks += [(kl_ref, r0, ATTN_KEY_BLOCK) for r0 in range(0, kl_ref.shape[0], ATTN_KEY_BLOCK)]

    def scores(qm, blk):
        ref, r0, n = blk
        return _dot_nt(ref[r0:r0 + n, :], qm)

    def col_max(blocks):
        return functools.reduce(jnp.maximum, [jnp.max(s, axis=0, keepdims=True) for s in blocks])

    def exp_block(s, m, slot, row0):
        e = jnp.exp2(s - m)
        e_ref[slot, row0:row0 + s.shape[0], :] = e.astype(BF16)
        return jnp.sum(e, axis=0, keepdims=True)

    def pv(slot):
        o = _dot(vtc_ref[...], e_ref[slot, 0:n_ctx, :])
        if has_lat:
            o = o + _dot(vtl_ref[...], e_ref[slot, n_ctx:, :])
        return o

    q1, q2 = jnp.where(lane < DA_QK_DIM, q, zero), jnp.where(lane >= DA_QK_DIM, q, zero)
    offsets = [sum(n for _, _, n in key_blocks[:i]) for i in range(len(key_blocks))]
    s1 = [scores(q1, blk) for blk in key_blocks]
    m1 = col_max(s1)
    s2, l1 = [], 0.0
    for blk, s, row0 in zip(key_blocks, s1, offsets):
        s2.append(scores(q2, blk))
        l1 = l1 + exp_block(s, m1, 0, row0)
    m2 = col_max(s2)
    o1 = pv(0)
    l2 = 0.0
    for s, row0 in zip(s2, offsets):
        l2 = l2 + exp_block(s, m2, 1, row0)
    ot = o1 / l1 - lam * (pv(1) / l2)
    ms = jnp.mean(ot * ot, axis=0, keepdims=True)
    o = (ot * lax.rsqrt(ms + EPS)).T
    o_ref[...] = (o * (gain_ref[...] * (1.0 - lambda_init))).astype(BF16)


def _diff_attention(q_lat, k_lat, vt_lat, q_ctx, k_ctx, vt_ctx, lqk, subln, lambda_init, batch, seq, ctx_len):
    w = q_lat.shape[1]
    gain = subln.reshape(1, w)
    small = [pl.BlockSpec(lqk.shape, lambda *_: (0, 0)),
             pl.BlockSpec((1, LANES), lambda b, h, *_: (0, h))]
    ctx_k = pl.BlockSpec((ctx_len, LANES), lambda b, h, *_: (b, h))
    ctx_vt = pl.BlockSpec((LANES, ctx_len), lambda b, h, *_: (h, b))
    lat_k = pl.BlockSpec((seq, LANES), lambda b, h, *_: (b, h))
    lat_vt = pl.BlockSpec((LANES, seq), lambda b, h, *_: (h, b))
    nq = seq // TQ
    q_tile = pl.BlockSpec((TQ, LANES), lambda b, h, i: (b * nq + i, h))
    o_lat = pl.pallas_call(
        functools.partial(_attn_kernel, lambda_init=lambda_init, has_lat=True),
        grid=(batch, DA_HEADS, nq),
        in_specs=small + [q_tile, ctx_k, ctx_vt, lat_k, lat_vt],
        out_specs=q_tile,
        out_shape=jax.ShapeDtypeStruct(q_lat.shape, BF16),
        scratch_shapes=[pltpu.VMEM((2, ctx_len + seq, TQ), BF16)],
        compiler_params=_params(3),
        name="diff_attn_latent",
    )(lqk, gain, q_lat, k_ctx, vt_ctx, k_lat, vt_lat)
    o_ctx = pl.pallas_call(
        functools.partial(_attn_kernel, lambda_init=lambda_init, has_lat=False),
        grid=(batch, DA_HEADS),
        in_specs=small + [ctx_k, ctx_k, ctx_vt],
        out_specs=ctx_k,
        out_shape=jax.ShapeDtypeStruct(q_ctx.shape, BF16),
        scratch_shapes=[pltpu.VMEM((2, ctx_len, ctx_len), BF16)],
        compiler_params=_params(2),
        name="diff_attn_context",
    )(lqk, gain, q_ctx, k_ctx, vt_ctx)
    return o_lat, o_ctx


def _pool_kernel(u_ref, pw_ref, ps_ref, out_ref):
    n = u_ref.shape[0]
    t = lax.broadcasted_iota(jnp.int32, (n, 1), 0)
    for g, w in enumerate(POOL_WINDOWS):
        sl = slice(g * LANES, (g + 1) * LANES)
        ug = u_ref[:, sl]
        lo, hi = -(w // 2), w - w // 2
        acc = ug
        for off in range(lo, hi):
            if off == 0:
                continue
            valid = (t + off >= 0) & (t + off < n)
            acc = acc + jnp.where(valid, pltpu.roll(ug, (-off) % n, axis=0), 0.0)
        cnt = (jnp.minimum(t + hi, n) - jnp.maximum(t + lo, 0)).astype(F32)
        dlt = acc / cnt - ug
        y = _dot(dlt.astype(BF16), pw_ref[g]) * ps_ref[:, sl]
        out_ref[:, sl] = y.astype(BF16)


def _pool(u, pool_w, pool_scale, seq_len, name):
    rows, w = u.shape
    return pl.pallas_call(
        _pool_kernel,
        grid=(rows // seq_len,),
        in_specs=[pl.BlockSpec((seq_len, w), lambda b: (b, 0)),
                  pl.BlockSpec(pool_w.shape, lambda b: (0, 0, 0)),
                  pl.BlockSpec((1, w), lambda b: (0, 0))],
        out_specs=pl.BlockSpec((seq_len, w), lambda b: (b, 0)),
        out_shape=jax.ShapeDtypeStruct((rows, w), BF16),
        compiler_params=_params(1),
        name=name,
    )(u, pool_w, pool_scale.reshape(1, w))


def _outproj_even_kernel(x_ref, mod_ref, o_ref, p_ref, w_ref, out_ref):
    k = o_ref.shape[1]
    y = _dot(o_ref[...], w_ref[0:k, :]) + _dot(p_ref[...], w_ref[k:, :])
    out_ref[...] = x_ref[...] + mod_ref[0][2:3] * y


def _outproj_even(xs, mods, o, p, w_out, tm, seq_len, mod_row, name):
    rows, d = xs.shape
    k = o.shape[1]
    return pl.pallas_call(
        _outproj_even_kernel,
        grid=(rows // tm,),
        in_specs=[pl.BlockSpec((tm, d), lambda i: (i, 0)), _mod_spec(d, tm, seq_len, mod_row),
                  pl.BlockSpec((tm, k), lambda i: (i, 0)),
                  pl.BlockSpec((tm, k), lambda i: (i, 0)),
                  _resident(w_out.shape)],
        out_specs=pl.BlockSpec((tm, d), lambda i: (i, 0)),
        out_shape=jax.ShapeDtypeStruct((rows, d), F32),
        compiler_params=_params(1),
        name=name,
    )(xs, mods, o, p, w_out)


def _ffn_kernel(x_ref, xp_ref, xn_ref, mod_ref, wu_ref, cw_ref, cb_ref, wd_ref, fin_ref, out_ref,
                *, tiles_per_seq, final):
    i = pl.program_id(0)
    tm = x_ref.shape[0]
    d_ff = wd_ref.shape[0]
    at_start = i % tiles_per_seq == 0
    at_end = (i + 1) % tiles_per_seq == 0
    m = mod_ref[0]
    shift, scale = m[3:4], m[4:5]
    x = x_ref[...]
    hp = jnp.where(at_start, 0.0, _modnorm(xp_ref[...], shift, scale))
    hn = jnp.where(at_end, 0.0, _modnorm(xn_ref[...], shift, scale))
    h = jnp.concatenate([hp, _modnorm(x, shift, scale), hn], axis=0).astype(BF16)
    ext = tm + 2 * SUBLANES

    def conv(u, c0):
        cw = cw_ref[:, c0:c0 + FF_CHUNK]
        y = (pltpu.roll(u, 1, axis=0) * cw[0:1] + u * cw[1:2] + pltpu.roll(u, ext - 1, axis=0) * cw[2:3]
             + cb_ref[:, c0:c0 + FF_CHUNK])
        return y[SUBLANES:SUBLANES + tm]

    acc = jnp.zeros((tm, out_ref.shape[1]), F32)
    for j in range(d_ff // FF_CHUNK):
        ca, cg = j * FF_CHUNK, d_ff + j * FF_CHUNK
        a = conv(_dot(h, wu_ref[:, ca:ca + FF_CHUNK]), ca)
        g = conv(_dot(h, wu_ref[:, cg:cg + FF_CHUNK]), cg)
        act = (a * (g * _sigmoid(g))).astype(BF16)
        acc = acc + _dot(act, wd_ref[ca:ca + FF_CHUNK, :])
    y = x + m[5:6] * acc
    if final:
        ms = jnp.mean(y * y, axis=-1, keepdims=True)
        y = y * lax.rsqrt(ms + EPS) * fin_ref[...]
    out_ref[...] = y


def _conv_ffn(xs, mods, w_up, conv_w, conv_b, w_down, final_norm, tm, seq_len, mod_row, final, name):
    rows, d = xs.shape
    tb = tm // SUBLANES
    last = rows // SUBLANES - 1
    return pl.pallas_call(
        functools.partial(_ffn_kernel, tiles_per_seq=seq_len // tm, final=final),
        grid=(rows // tm,),
        in_specs=[pl.BlockSpec((tm, d), lambda i: (i, 0)),
                  pl.BlockSpec((SUBLANES, d), lambda i: (jnp.maximum(i * tb - 1, 0), 0)),
                  pl.BlockSpec((SUBLANES, d), lambda i: (jnp.minimum((i + 1) * tb, last), 0)),
                  _mod_spec(d, tm, seq_len, mod_row),
                  _resident(w_up.shape), _resident(conv_w.shape), _resident((1, conv_b.shape[0])),
                  _resident(w_down.shape), _resident((1, d))],
        out_specs=pl.BlockSpec((tm, d), lambda i: (i, 0)),
        out_shape=jax.ShapeDtypeStruct((rows, d), F32),
        compiler_params=_params(1),
        name=name,
    )(xs, xs, xs, mods, w_up, conv_w, conv_b.reshape(1, -1), w_down, final_norm.reshape(1, d))


def _inproj_odd_kernel(x_ref, mod_ref, w_ref, *out_refs):
    m = mod_ref[0]
    h = _modnorm(x_ref[...], m[0:1], m[1:2]).astype(BF16)
    w = out_refs[0].shape[1]
    step = 512
    for c0 in range(0, w, step):
        sl = slice(c0, c0 + step)
        for n, ref in enumerate(out_refs):
            p = _dot(h, w_ref[:, n * w + c0:n * w + c0 + step])
            if n == 0:
                p = p * _sigmoid(p)
            ref[:, sl] = p.astype(ref.dtype)


def _inproj_odd(xs, mods, w_in, tm, seq_len, mod_row, want_gate, name):
    rows, d = xs.shape
    w = w_in.shape[1] // 5
    dtypes = [BF16, BF16, F32, F32] + ([BF16] if want_gate else [])
    return pl.pallas_call(
        _inproj_odd_kernel,
        grid=(rows // tm,),
        in_specs=[pl.BlockSpec((tm, d), lambda i: (i, 0)), _mod_spec(d, tm, seq_len, mod_row),
                  _resident(w_in.shape)],
        out_specs=[pl.BlockSpec((tm, w), lambda i: (i, 0))] * len(dtypes),
        out_shape=[jax.ShapeDtypeStruct((rows, w), dt) for dt in dtypes],
        compiler_params=_params(1),
        name=name,
    )(xs, mods, w_in)


def _cumsum_rows(tri, x):
    w = x.shape[1]
    hi = x.astype(BF16)
    r1 = x - hi.astype(F32)
    mid = r1.astype(BF16)
    lo = (r1 - mid.astype(F32)).astype(BF16)
    s = _dot(tri, jnp.concatenate([hi, mid, lo], axis=1))
    return s[:, 0:w] + s[:, w:2 * w] + s[:, 2 * w:3 * w]


def _hgrn2_kernel(lbl_ref, qc_ref, vc_ref, zfc_ref, zbc_ref, ql_ref, vl_ref, zfl_ref, zbl_ref,
                  o_ref, ob_ref, st_ref, ds_ref, *, layer):
    c = HG_CHUNK
    rb = HG_BLOCK
    n_sub = rb // c
    w = o_ref.shape[1]
    heads = w // HG_HEAD_DIM
    n_ctx = qc_ref.shape[0] // rb
    n_lat = ql_ref.shape[0] // rb
    r_i = lax.broadcasted_iota(jnp.int32, (rb, rb), 0)
    c_i = lax.broadcasted_iota(jnp.int32, (rb, rb), 1)
    same_chunk = (r_i // c) == (c_i // c)

    def direction_constants(d):
        fwd = d == 0
        lg = lbl_ref[d]
        e = jnp.exp(lg - jnp.max(lg, axis=0, keepdims=True))
        p = e / jnp.sum(e, axis=0, keepdims=True)
        lb = jnp.sum(p[1:layer + 1], axis=0, keepdims=True)
        causal = same_chunk & ((c_i <= r_i) if fwd else (c_i >= r_i))
        tri = jnp.where(causal, 1.0, 0.0).astype(BF16)
        ref_row = c // 2 - 1 if fwd else c // 2
        end_row = c - 1 if fwd else 0
        order = range(n_sub) if fwd else range(n_sub - 1, -1, -1)
        return lb, causal, tri, ref_row, end_row, order

    consts = [direction_constants(0), direction_constants(1)]

    def block(d, q, v, z, want_out, out):
        lb, causal, tri, ref_row, end_row, order = consts[d]
        split = lambda a: a.reshape(n_sub, c, w)
        merge = lambda a: a.reshape(rb, w).astype(BF16)
        hsl = [slice(h * HG_HEAD_DIM, (h + 1) * HG_HEAD_DIM) for h in range(heads)]
        csl = [slice(j * c, (j + 1) * c) for j in range(n_sub)]
        f = lb + (1.0 - lb) * _sigmoid(z)
        k = split(1.0 - f)
        logf = jnp.log(f)
        yield
        b = split(_cumsum_rows(tri, logf))
        yield
        r = b[:, ref_row:ref_row + 1, :]
        be = b[:, end_row:end_row + 1, :]
        kh = merge(k * jnp.exp(be - b))
        decay = jnp.exp(be)
        if want_out:
            qf = split(q.astype(F32))
            qt = merge(qf * jnp.exp(b - r))
            kt = merge(k * jnp.exp(r - b))
            qh = merge(qf * jnp.exp(b))
        yield
        for h in range(heads):
            for j in range(n_sub):
                ds_ref[d, h, j] = _dot_tn(v[csl[j], hsl[h]], kh[csl[j], hsl[h]])
        if want_out:
            a_raw = [_dot_nt(qt[:, sl], kt[:, sl]) for sl in hsl]
        yield
        states = []
        for h in range(heads):
            st = st_ref[d, h]
            before = [None] * n_sub
            for j in order:
                before[j] = st.astype(BF16)
                st = st * decay[j][:, hsl[h]] + ds_ref[d, h, j]
            st_ref[d, h] = st
            states.append(before)
        if want_out:
            a = [jnp.where(causal, x, 0.0).astype(BF16) for x in a_raw]
        yield
        if want_out:
            outs = []
            for h in range(heads):
                inter = [_dot_nt(qh[csl[j], hsl[h]], states[h][j]) for j in range(n_sub)]
                outs.append(_dot(a[h], v[:, hsl[h]]) + jnp.concatenate(inter, axis=0))
            out[d] = jnp.concatenate(outs, axis=1)
        yield

    def run_interleaved(gens):
        for _ in zip(*gens):
            pass

    def block_rows(d, j, n):
        return pl.ds(pl.multiple_of((j if d == 0 else n - 1 - j) * rb, rb), rb)

    def ctx_step(j, carry):
        gens = []
        for d, z_ref in ((0, zfc_ref), (1, zbc_ref)):
            rows = block_rows(d, j, n_ctx)
            gens.append(block(d, qc_ref[rows, :], vc_ref[rows, :], z_ref[rows, :], False, None))
        run_interleaved(gens)
        return carry

    def lat_step(j, carry):
        out = {}
        rows = [block_rows(d, j, n_lat) for d in (0, 1)]
        gens = [block(d, ql_ref[rows[d], :], vl_ref[rows[d], :], z_ref[rows[d], :], True, out)
                for d, z_ref in ((0, zfl_ref), (1, zbl_ref))]
        run_interleaved(gens)
        o_ref[rows[0], :] = out[0]
        ob_ref[rows[1], :] = out[1]
        return carry

    st_ref[...] = jnp.zeros_like(st_ref)
    lax.fori_loop(0, n_ctx, ctx_step, 0)
    lax.fori_loop(0, n_lat, lat_step, 0)
    o_ref[...] += ob_ref[...]


def _hgrn2(lat_in, ctx_in, lb_logits, layer, batch, seq, ctx_len):
    w = lat_in[0].shape[1]
    ctx = pl.BlockSpec((ctx_len, HG_LANES), lambda b, g: (b, g))
    lat = pl.BlockSpec((seq, HG_LANES), lambda b, g: (b, g))
    heads = HG_LANES // HG_HEAD_DIM
    return pl.pallas_call(
        functools.partial(_hgrn2_kernel, layer=layer),
        grid=(batch, w // HG_LANES),
        in_specs=[pl.BlockSpec((2, lb_logits.shape[1], HG_LANES), lambda b, g: (0, 0, g))] + [ctx] * 4 + [lat] * 4,
        out_specs=lat,
        out_shape=jax.ShapeDtypeStruct((batch * seq, w), F32),
        scratch_shapes=[pltpu.VMEM((seq, HG_LANES), F32),
                        pltpu.VMEM((2, heads, HG_HEAD_DIM, HG_HEAD_DIM), F32),
                        pltpu.VMEM((2, heads, HG_BLOCK // HG_CHUNK, HG_HEAD_DIM, HG_HEAD_DIM), F32)],
        compiler_params=_params(2),
        name="hgrn2_scan",
    )(lb_logits, *ctx_in, *lat_in)


def _outproj_odd_kernel(x_ref, mod_ref, o_ref, g_ref, gain_ref, w_ref, out_ref, hn_ref):
    for h in range(o_ref.shape[1] // HG_HEAD_DIM):
        sl = slice(h * HG_HEAD_DIM, (h + 1) * HG_HEAD_DIM)
        o = o_ref[:, sl]
        g = g_ref[:, sl].astype(F32)
        ms = jnp.mean(o * o, axis=-1, keepdims=True)
        hn_ref[:, sl] = (o * lax.rsqrt(ms + EPS) * gain_ref[:, sl] * (g * _sigmoid(g))).astype(BF16)
    out_ref[...] = x_ref[...] + mod_ref[0][2:3] * _dot(hn_ref[...], w_ref[...])


def _outproj_odd(xs, mods, o, g, gain, w_out, tm, seq_len):
    rows, w = o.shape
    d = xs.shape[1]
    return pl.pallas_call(
        _outproj_odd_kernel,
        grid=(rows // tm,),
        in_specs=[pl.BlockSpec((tm, d), lambda i: (i, 0)), _mod_spec(d, tm, seq_len, None),
                  pl.BlockSpec((tm, w), lambda i: (i, 0)),
                  pl.BlockSpec((tm, w), lambda i: (i, 0)),
                  _resident((1, w)), _resident(w_out.shape)],
        out_specs=pl.BlockSpec((tm, d), lambda i: (i, 0)),
        out_shape=jax.ShapeDtypeStruct((rows, d), F32),
        scratch_shapes=[pltpu.VMEM((tm, w), BF16)],
        compiler_params=_params(1),
        name="odd_outproj",
    )(xs, mods, o, g, gain.reshape(1, w), w_out)


def kernel(x, c, ctx, c_ctx, w_mod, b_mod, ev_w_in, ev_w_out, da_lq1, da_lk1, da_lq2, da_lk2, da_subln, pool_w,
           pool_scale, hg_w_in, hg_w_out, hg_lb_logits, hg_norm, ffn_w_up, ffn_conv_w, ffn_conv_b, ffn_w_down,
           final_norm):
    batch, seq, d = x.shape
    ctx_len = ctx.shape[1]
    depth = w_mod.shape[0]
    assert seq % TM_LAT == 0 and seq % TQ == 0 and seq % HG_BLOCK == 0
    assert ctx_len % TM_CTX == 0 and ctx_len % HG_BLOCK == 0

    xl = x.reshape(batch * seq, d)
    xc = ctx.reshape(batch * ctx_len, d)
    cond_rows = 2 * SUBLANES
    cond = jnp.zeros((cond_rows, d), F32).at[:batch].set(c).at[batch].set(c_ctx)
    mods_all = _modulation(cond, w_mod, b_mod).reshape(depth, cond_rows, 6, d)
    lat = dict(tm=TM_LAT, seq_len=seq, mod_row=None)
    cxt = dict(tm=TM_CTX, seq_len=ctx_len, mod_row=batch)

    for i in range(depth):
        last = i == depth - 1
        mods = mods_all[i]
        j = i // 2
        ffn_w = (ffn_w_up[i].astype(BF16), ffn_conv_w[i], ffn_conv_b[i], ffn_w_down[i].astype(BF16), final_norm)
        if i % 2 == 0:
            lambda_init = 0.8 - 0.6 * math.exp(-0.3 * i)
            wq = ev_w_in[j].shape[1] // 4
            w_qku = jnp.concatenate([ev_w_in[j][:, :2 * wq], ev_w_in[j][:, 3 * wq:]], axis=1).astype(BF16)
            w_vt = ev_w_in[j][:, 2 * wq:3 * wq].T.astype(BF16)
            w_out = ev_w_out[j].astype(BF16)
            pw = pool_w[j].astype(BF16)
            lqk = jnp.stack([da_lq1[j], da_lk1[j], da_lq2[j], da_lk2[j]]).astype(F32)
            ql, kl, vtl, ul = _inproj_even(xl, mods, w_qku, w_vt, _rope_tables(seq), name="even_inproj_latent", **lat)
            qc, kc, vtc, uc = _inproj_even(xc, mods, w_qku, w_vt, None, name="even_inproj_context", **cxt)
            ol, oc = _diff_attention(ql, kl, vtl, qc, kc, vtc, lqk, da_subln[j], lambda_init, batch, seq, ctx_len)
            xl = _outproj_even(xl, mods, ol, _pool(ul, pw, pool_scale[j], seq, "pool_latent"), w_out,
                               name="even_outproj_latent", **lat)
            if not last:
                xc = _outproj_even(xc, mods, oc, _pool(uc, pw, pool_scale[j], ctx_len, "pool_context"), w_out,
                                   name="even_outproj_context", **cxt)
        else:
            if not last:
                raise NotImplementedError("context readout of an HGRN2 layer that is not the last layer")
            w_in = hg_w_in[j].astype(BF16)
            *lat_in, gl = _inproj_odd(xl, mods, w_in, want_gate=True, name="odd_inproj_latent", **lat)
            ctx_in = _inproj_odd(xc, mods, w_in, want_gate=False, name="odd_inproj_context", **cxt)
            o = _hgrn2(lat_in, ctx_in, hg_lb_logits.astype(F32), i, batch, seq, ctx_len)
            xl = _outproj_odd(xl, mods, o, gl, hg_norm[j], hg_w_out[j].astype(BF16), TM_LAT, seq)
        xl = _conv_ffn(xl, mods, *ffn_w, final=last, name="conv_ffn_latent", **lat)
        if not last:
            xc = _conv_ffn(xc, mods, *ffn_w, final=False, name="conv_ffn_context", **cxt)
    return xl.reshape(batch, seq, d)
```

```python
import functools
import math

import jax
import jax.numpy as jnp
from jax import lax
from jax.experimental import pallas as pl
from jax.experimental.pallas import tpu as pltpu

F32 = jnp.float32
BF16 = jnp.bfloat16

EPS = 1e-6
GRID_W = 64
DA_HEADS = 4
DA_QK_DIM = 64
ROPE_THETA = 10000.0
POOL_WINDOWS = (2, 4, 8, 16)
HG_HEAD_DIM = 128
LANES = 128
SUBLANES = 8
VMEM_LIMIT = 56 * 1024 * 1024

TM_LAT = 512
TM_CTX = 256
TQ = 256
ATTN_KEY_BLOCK = 256
HG_CHUNK = 32
HG_LANES = 256
HG_BLOCK = 256
FF_CHUNK = 1408
LOG2E = math.log2(math.e)


def _resident(shape):
    nd = len(shape)
    return pl.BlockSpec(shape, lambda *_: (0,) * nd, pipeline_mode=pl.Buffered(1))


def _params(n_axes):
    return pltpu.CompilerParams(dimension_semantics=("parallel",) * n_axes, vmem_limit_bytes=VMEM_LIMIT)


def _mod_spec(d, tm, seq_len, fixed_row):
    if fixed_row is None:
        tiles_per_seq = seq_len // tm
        return pl.BlockSpec((1, 6, d), lambda i: (i // tiles_per_seq, 0, 0))
    return pl.BlockSpec((1, 6, d), lambda i: (fixed_row, 0, 0))


def _sigmoid(x):
    return 1.0 / (1.0 + jnp.exp(-x))


def _modnorm(x, shift, scale):
    ms = jnp.mean(x * x, axis=-1, keepdims=True)
    return (x * lax.rsqrt(ms + EPS)) * (1.0 + scale) + shift


def _dot(a, b):
    return jnp.dot(a, b, preferred_element_type=F32)


def _dot_nt(a, b):
    return lax.dot_general(a, b, (((1,), (1,)), ((), ())), preferred_element_type=F32)


def _dot_tn(a, b):
    return lax.dot_general(a, b, (((0,), (0,)), ((), ())), preferred_element_type=F32)


def _mod_kernel(cond_ref, w_ref, b_ref, out_ref):
    c = cond_ref[...]
    s = c * _sigmoid(c)
    out_ref[0] = jnp.dot(s, w_ref[0], precision=lax.Precision.HIGHEST,
                         preferred_element_type=F32) + b_ref[0]


def _modulation(cond, w_mod, b_mod):
    depth, d, n = w_mod.shape
    rows = cond.shape[0]
    tn = 1024
    return pl.pallas_call(
        _mod_kernel,
        grid=(depth, n // tn),
        in_specs=[pl.BlockSpec((rows, d), lambda l, j: (0, 0)),
                  pl.BlockSpec((1, d, tn), lambda l, j: (l, 0, j)),
                  pl.BlockSpec((1, 1, tn), lambda l, j: (l, 0, j))],
        out_specs=pl.BlockSpec((1, rows, tn), lambda l, j: (l, 0, j)),
        out_shape=jax.ShapeDtypeStruct((depth, rows, n), F32),
        compiler_params=_params(2),
        name="adaln_mod",
    )(cond, w_mod, b_mod.reshape(depth, 1, n))


def _inproj_even_kernel(*refs, has_rope):
    if has_rope:
        x_ref, mod_ref, w_ref, wvt_ref, cos_ref, sa_ref, sb_ref, q_ref, k_ref, vt_ref, u_ref = refs
    else:
        x_ref, mod_ref, w_ref, wvt_ref, q_ref, k_ref, vt_ref, u_ref = refs
    m = mod_ref[0]
    h = _modnorm(x_ref[...], m[0:1], m[1:2]).astype(BF16)
    w = q_ref.shape[1]
    pq = _dot(h, w_ref[:, 0:w])
    pk = _dot(h, w_ref[:, w:2 * w])
    if has_rope:
        reps = w // LANES
        cos = jnp.concatenate([cos_ref[...]] * reps, axis=1)
        sa = jnp.concatenate([sa_ref[...]] * reps, axis=1)
        sb = jnp.concatenate([sb_ref[...]] * reps, axis=1)
        half = DA_QK_DIM // 4

        def rope(p):
            return p * cos + pltpu.roll(p, w - half, axis=1) * sa + pltpu.roll(p, half, axis=1) * sb

        pq, pk = rope(pq), rope(pk)
    q_ref[...] = (pq * (DA_QK_DIM ** -0.5 * LOG2E)).astype(BF16)
    k_ref[...] = pk.astype(BF16)
    vt_ref[...] = _dot_nt(wvt_ref[...], h).astype(BF16)
    u_ref[...] = _dot(h, w_ref[:, 2 * w:3 * w])


def _inproj_even(xs, mods, w_qku, w_vt, rope_tabs, tm, seq_len, mod_row, name):
    rows, d = xs.shape
    w = w_vt.shape[0]
    has_rope = rope_tabs is not None
    in_specs = [pl.BlockSpec((tm, d), lambda i: (i, 0)), _mod_spec(d, tm, seq_len, mod_row),
                _resident(w_qku.shape), _resident(w_vt.shape)]
    args = [xs, mods, w_qku, w_vt]
    if has_rope:
        tiles_per_seq = seq_len // tm
        in_specs += [pl.BlockSpec((tm, LANES), lambda i: (i % tiles_per_seq, 0))] * 3
        args += list(rope_tabs)
    tok = lambda dt: jax.ShapeDtypeStruct((rows, w), dt)
    tok_spec = pl.BlockSpec((tm, w), lambda i: (i, 0))
    return pl.pallas_call(
        functools.partial(_inproj_even_kernel, has_rope=has_rope),
        grid=(rows // tm,),
        in_specs=in_specs,
        out_specs=[tok_spec, tok_spec, pl.BlockSpec((w, tm), lambda i: (0, i)), tok_spec],
        out_shape=[tok(BF16), tok(BF16), jax.ShapeDtypeStruct((w, rows), BF16), tok(F32)],
        compiler_params=_params(1),
        name=name,
    )(*args)


def _rope_tables(seq):
    rows_n = seq // GRID_W
    row = jnp.repeat(jnp.arange(rows_n, dtype=F32), GRID_W)
    col = jnp.tile(jnp.arange(GRID_W, dtype=F32), rows_n)
    n_freq = DA_QK_DIM // 4
    inv = ROPE_THETA ** (-jnp.arange(n_freq, dtype=F32) / n_freq)
    ang_r, ang_c = row[:, None] * inv, col[:, None] * inv
    zeros = jnp.zeros_like(ang_r)
    cos64 = jnp.concatenate([jnp.cos(ang_r)] * 2 + [jnp.cos(ang_c)] * 2, axis=1)
    sa64 = jnp.concatenate([-jnp.sin(ang_r), zeros, -jnp.sin(ang_c), zeros], axis=1)
    sb64 = jnp.concatenate([zeros, jnp.sin(ang_r), zeros, jnp.sin(ang_c)], axis=1)
    return tuple(jnp.concatenate([t, t], axis=1) for t in (cos64, sa64, sb64))


def _attn_kernel(*refs, lambda_init, has_lat):
    if has_lat:
        lqk_ref, gain_ref, q_ref, kc_ref, vtc_ref, kl_ref, vtl_ref, o_ref, s_ref = refs
    else:
        lqk_ref, gain_ref, q_ref, kc_ref, vtc_ref, o_ref, s_ref = refs
    tq = s_ref.shape[3]
    nq = q_ref.shape[0] // tq
    n_ctx = kc_ref.shape[0]
    a = lqk_ref[...]
    lam = (jnp.exp(jnp.sum(a[0:1] * a[1:2], axis=-1, keepdims=True))
           - jnp.exp(jnp.sum(a[2:3] * a[3:4], axis=-1, keepdims=True)) + lambda_init)
    lane = lax.broadcasted_iota(jnp.int32, (1, LANES), 1)
    map_lanes = [lane < DA_QK_DIM, lane >= DA_QK_DIM]
    gain = gain_ref[...] * (1.0 - lambda_init)

    def tile_rows(t):
        return pl.ds(pl.multiple_of(t * tq, tq), tq)

    def scores(t, slot):
        q = q_ref[tile_rows(t), :]
        for i, sel in enumerate(map_lanes):
            qm = jnp.where(sel, q, jnp.zeros_like(q))
            s_ref[slot, i, 0:n_ctx, :] = _dot_nt(kc_ref[...], qm)
            if has_lat:
                s_ref[slot, i, n_ctx:, :] = _dot_nt(kl_ref[...], qm)

    def softmax_pv(t, slot):
        outs = []
        for i in range(2):
            s = s_ref[slot, i]
            e = jnp.exp2(s - jnp.max(s, axis=0, keepdims=True))
            l = jnp.sum(e, axis=0, keepdims=True)
            e = e.astype(BF16)
            o = _dot(vtc_ref[...], e[0:n_ctx])
            if has_lat:
                o = o + _dot(vtl_ref[...], e[n_ctx:])
            outs.append(o / l)
        ot = outs[0] - lam * outs[1]
        ms = jnp.mean(ot * ot, axis=0, keepdims=True)
        o_ref[tile_rows(t), :] = ((ot * lax.rsqrt(ms + EPS)).T * gain).astype(BF16)

    scores(0, 0)
    if nq > 1:
        def tile_pair(i, carry):
            t = 2 * i
            scores(t + 1, 1)
            softmax_pv(t, 0)
            scores(t + 2, 0)
            softmax_pv(t + 1, 1)
            return carry

        lax.fori_loop(0, nq // 2 - 1, tile_pair, 0)
        scores(nq - 1, 1)
        softmax_pv(nq - 2, 0)
        softmax_pv(nq - 1, 1)
    else:
        softmax_pv(0, 0)


def _diff_attention(q_lat, k_lat, vt_lat, q_ctx, k_ctx, vt_ctx, lqk, subln, lambda_init, batch, seq, ctx_len):
    w = q_lat.shape[1]
    assert (seq // TQ) % 2 == 0
    gain = subln.reshape(1, w)
    small = [pl.BlockSpec(lqk.shape, lambda b, h: (0, 0)),
             pl.BlockSpec((1, LANES), lambda b, h: (0, h))]
    ctx_k = pl.BlockSpec((ctx_len, LANES), lambda b, h: (b, h))
    ctx_vt = pl.BlockSpec((LANES, ctx_len), lambda b, h: (h, b))
    lat_k = pl.BlockSpec((seq, LANES), lambda b, h: (b, h))
    lat_vt = pl.BlockSpec((LANES, seq), lambda b, h: (h, b))
    o_lat = pl.pallas_call(
        functools.partial(_attn_kernel, lambda_init=lambda_init, has_lat=True),
        grid=(batch, DA_HEADS),
        in_specs=small + [lat_k, ctx_k, ctx_vt, lat_k, lat_vt],
        out_specs=lat_k,
        out_shape=jax.ShapeDtypeStruct(q_lat.shape, BF16),
        scratch_shapes=[pltpu.VMEM((2, 2, ctx_len + seq, TQ), F32)],
        compiler_params=_params(2),
        name="diff_attn_latent",
    )(lqk, gain, q_lat, k_ctx, vt_ctx, k_lat, vt_lat)
    o_ctx = pl.pallas_call(
        functools.partial(_attn_kernel, lambda_init=lambda_init, has_lat=False),
        grid=(batch, DA_HEADS),
        in_specs=small + [ctx_k, ctx_k, ctx_vt],
        out_specs=ctx_k,
        out_shape=jax.ShapeDtypeStruct(q_ctx.shape, BF16),
        scratch_shapes=[pltpu.VMEM((1, 2, ctx_len, ctx_len), F32)],
        compiler_params=_params(2),
        name="diff_attn_context",
    )(lqk, gain, q_ctx, k_ctx, vt_ctx)
    return o_lat, o_ctx


def _pool_kernel(u_ref, pw_ref, ps_ref, out_ref):
    n = u_ref.shape[0]
    t = lax.broadcasted_iota(jnp.int32, (n, 1), 0)
    for g, w in enumerate(POOL_WINDOWS):
        sl = slice(g * LANES, (g + 1) * LANES)
        ug = u_ref[:, sl]
        lo, hi = -(w // 2), w - w // 2
        acc = ug
        for off in range(lo, hi):
            if off == 0:
                continue
            valid = (t + off >= 0) & (t + off < n)
            acc = acc + jnp.where(valid, pltpu.roll(ug, (-off) % n, axis=0), 0.0)
        cnt = (jnp.minimum(t + hi, n) - jnp.maximum(t + lo, 0)).astype(F32)
        dlt = acc / cnt - ug
        y = _dot(dlt.astype(BF16), pw_ref[g]) * ps_ref[:, sl]
        out_ref[:, sl] = y.astype(BF16)


def _pool(u, pool_w, pool_scale, seq_len, name):
    rows, w = u.shape
    return pl.pallas_call(
        _pool_kernel,
        grid=(rows // seq_len,),
        in_specs=[pl.BlockSpec((seq_len, w), lambda b: (b, 0)),
                  pl.BlockSpec(pool_w.shape, lambda b: (0, 0, 0)),
                  pl.BlockSpec((1, w), lambda b: (0, 0))],
        out_specs=pl.BlockSpec((seq_len, w), lambda b: (b, 0)),
        out_shape=jax.ShapeDtypeStruct((rows, w), BF16),
        compiler_params=_params(1),
        name=name,
    )(u, pool_w, pool_scale.reshape(1, w))


def _outproj_even_kernel(x_ref, mod_ref, o_ref, p_ref, w_ref, out_ref):
    k = o_ref.shape[1]
    y = _dot(o_ref[...], w_ref[0:k, :]) + _dot(p_ref[...], w_ref[k:, :])
    out_ref[...] = x_ref[...] + mod_ref[0][2:3] * y


def _outproj_even(xs, mods, o, p, w_out, tm, seq_len, mod_row, name):
    rows, d = xs.shape
    k = o.shape[1]
    return pl.pallas_call(
        _outproj_even_kernel,
        grid=(rows // tm,),
        in_specs=[pl.BlockSpec((tm, d), lambda i: (i, 0)), _mod_spec(d, tm, seq_len, mod_row),
                  pl.BlockSpec((tm, k), lambda i: (i, 0)),
                  pl.BlockSpec((tm, k), lambda i: (i, 0)),
                  _resident(w_out.shape)],
        out_specs=pl.BlockSpec((tm, d), lambda i: (i, 0)),
        out_shape=jax.ShapeDtypeStruct((rows, d), F32),
        compiler_params=_params(1),
        name=name,
    )(xs, mods, o, p, w_out)


def _ffn_kernel(x_ref, xp_ref, xn_ref, mod_ref, wu_ref, cw_ref, cb_ref, wd_ref, fin_ref, out_ref,
                *, tiles_per_seq, final):
    i = pl.program_id(0)
    tm = x_ref.shape[0]
    d_ff = wd_ref.shape[0]
    at_start = i % tiles_per_seq == 0
    at_end = (i + 1) % tiles_per_seq == 0
    m = mod_ref[0]
    shift, scale = m[3:4], m[4:5]
    x = x_ref[...]
    hp = jnp.where(at_start, 0.0, _modnorm(xp_ref[...], shift, scale))
    hn = jnp.where(at_end, 0.0, _modnorm(xn_ref[...], shift, scale))
    h = jnp.concatenate([hp, _modnorm(x, shift, scale), hn], axis=0).astype(BF16)
    ext = tm + 2 * SUBLANES

    def conv(u, c0):
        cw = cw_ref[:, c0:c0 + FF_CHUNK]
        y = (pltpu.roll(u, 1, axis=0) * cw[0:1] + u * cw[1:2] + pltpu.roll(u, ext - 1, axis=0) * cw[2:3]
             + cb_ref[:, c0:c0 + FF_CHUNK])
        return y[SUBLANES:SUBLANES + tm]

    acc = jnp.zeros((tm, out_ref.shape[1]), F32)
    for j in range(d_ff // FF_CHUNK):
        ca, cg = j * FF_CHUNK, d_ff + j * FF_CHUNK
        a = conv(_dot(h, wu_ref[:, ca:ca + FF_CHUNK]), ca)
        g = conv(_dot(h, wu_ref[:, cg:cg + FF_CHUNK]), cg)
        act = (a * (g * _sigmoid(g))).astype(BF16)
        acc = acc + _dot(act, wd_ref[ca:ca + FF_CHUNK, :])
    y = x + m[5:6] * acc
    if final:
        ms = jnp.mean(y * y, axis=-1, keepdims=True)
        y = y * lax.rsqrt(ms + EPS) * fin_ref[...]
    out_ref[...] = y


def _conv_ffn(xs, mods, w_up, conv_w, conv_b, w_down, final_norm, tm, seq_len, mod_row, final, name):
    rows, d = xs.shape
    tb = tm // SUBLANES
    last = rows // SUBLANES - 1
    return pl.pallas_call(
        functools.partial(_ffn_kernel, tiles_per_seq=seq_len // tm, final=final),
        grid=(rows // tm,),
        in_specs=[pl.BlockSpec((tm, d), lambda i: (i, 0)),
                  pl.BlockSpec((SUBLANES, d), lambda i: (jnp.maximum(i * tb - 1, 0), 0)),
                  pl.BlockSpec((SUBLANES, d), lambda i: (jnp.minimum((i + 1) * tb, last), 0)),
                  _mod_spec(d, tm, seq_len, mod_row),
                  _resident(w_up.shape), _resident(conv_w.shape), _resident((1, conv_b.shape[0])),
                  _resident(w_down.shape), _resident((1, d))],
        out_specs=pl.BlockSpec((tm, d), lambda i: (i, 0)),
        out_shape=jax.ShapeDtypeStruct((rows, d), F32),
        compiler_params=_params(1),
        name=name,
    )(xs, xs, xs, mods, w_up, conv_w, conv_b.reshape(1, -1), w_down, final_norm.reshape(1, d))


def _inproj_odd_kernel(x_ref, mod_ref, w_ref, *out_refs):
    m = mod_ref[0]
    h = _modnorm(x_ref[...], m[0:1], m[1:2]).astype(BF16)
    w = out_refs[0].shape[1]
    step = 512
    for c0 in range(0, w, step):
        sl = slice(c0, c0 + step)
        for n, ref in enumerate(out_refs):
            p = _dot(h, w_ref[:, n * w + c0:n * w + c0 + step])
            if n == 0:
                p = p * _sigmoid(p)
            ref[:, sl] = p.astype(ref.dtype)


def _inproj_odd(xs, mods, w_in, tm, seq_len, mod_row, want_gate, name):
    rows, d = xs.shape
    w = w_in.shape[1] // 5
    dtypes = [BF16, BF16, F32, F32] + ([BF16] if want_gate else [])
    return pl.pallas_call(
        _inproj_odd_kernel,
        grid=(rows // tm,),
        in_specs=[pl.BlockSpec((tm, d), lambda i: (i, 0)), _mod_spec(d, tm, seq_len, mod_row),
                  _resident(w_in.shape)],
        out_specs=[pl.BlockSpec((tm, w), lambda i: (i, 0))] * len(dtypes),
        out_shape=[jax.ShapeDtypeStruct((rows, w), dt) for dt in dtypes],
        compiler_params=_params(1),
        name=name,
    )(xs, mods, w_in)


def _cumsum_rows(tri, x):
    w = x.shape[1]
    hi = x.astype(BF16)
    r1 = x - hi.astype(F32)
    mid = r1.astype(BF16)
    lo = (r1 - mid.astype(F32)).astype(BF16)
    s = _dot(tri, jnp.concatenate([hi, mid, lo], axis=1))
    return s[:, 0:w] + s[:, w:2 * w] + s[:, 2 * w:3 * w]


def _hgrn2_kernel(lbl_ref, qc_ref, vc_ref, zfc_ref, zbc_ref, ql_ref, vl_ref, zfl_ref, zbl_ref,
                  o_ref, ob_ref, st_ref, ds_ref, *, layer):
    c = HG_CHUNK
    rb = HG_BLOCK
    n_sub = rb // c
    w = o_ref.shape[1]
    heads = w // HG_HEAD_DIM
    n_ctx = qc_ref.shape[0] // rb
    n_lat = ql_ref.shape[0] // rb
    r_i = lax.broadcasted_iota(jnp.int32, (rb, rb), 0)
    c_i = lax.broadcasted_iota(jnp.int32, (rb, rb), 1)
    same_chunk = (r_i // c) == (c_i // c)

    def direction_constants(d):
        fwd = d == 0
        lg = lbl_ref[d]
        e = jnp.exp(lg - jnp.max(lg, axis=0, keepdims=True))
        p = e / jnp.sum(e, axis=0, keepdims=True)
        lb = jnp.sum(p[1:layer + 1], axis=0, keepdims=True)
        causal = same_chunk & ((c_i <= r_i) if fwd else (c_i >= r_i))
        tri = jnp.where(causal, 1.0, 0.0).astype(BF16)
        ref_row = c // 2 - 1 if fwd else c // 2
        end_row = c - 1 if fwd else 0
        order = range(n_sub) if fwd else range(n_sub - 1, -1, -1)
        return lb, causal, tri, ref_row, end_row, order

    consts = [direction_constants(0), direction_constants(1)]

    def block(d, q, v, z, want_out, out):
        lb, causal, tri, ref_row, end_row, order = consts[d]
        split = lambda a: a.reshape(n_sub, c, w)
        merge = lambda a: a.reshape(rb, w).astype(BF16)
        hsl = [slice(h * HG_HEAD_DIM, (h + 1) * HG_HEAD_DIM) for h in range(heads)]
        csl = [slice(j * c, (j + 1) * c) for j in range(n_sub)]
        f = lb + (1.0 - lb) * _sigmoid(z)
        k = split(1.0 - f)
        logf = jnp.log(f)
        yield
        b = split(_cumsum_rows(tri, logf))
        yield
        r = b[:, ref_row:ref_row + 1, :]
        be = b[:, end_row:end_row + 1, :]
        kh = merge(k * jnp.exp(be - b))
        decay = jnp.exp(be)
        if want_out:
            qf = split(q.astype(F32))
            qt = merge(qf * jnp.exp(b - r))
            kt = merge(k * jnp.exp(r - b))
            qh = merge(qf * jnp.exp(b))
        yield
        for h in range(heads):
            for j in range(n_sub):
                ds_ref[d, h, j] = _dot_tn(v[csl[j], hsl[h]], kh[csl[j], hsl[h]])
        if want_out:
            a_raw = [_dot_nt(qt[:, sl], kt[:, sl]) for sl in hsl]
        yield
        states = []
        for h in range(heads):
            st = st_ref[d, h]
            before = [None] * n_sub
            for j in order:
                before[j] = st.astype(BF16)
                st = st * decay[j][:, hsl[h]] + ds_ref[d, h, j]
            st_ref[d, h] = st
            states.append(before)
        if want_out:
            a = [jnp.where(causal, x, 0.0).astype(BF16) for x in a_raw]
        yield
        if want_out:
            outs = []
            for h in range(heads):
                inter = [_dot_nt(qh[csl[j], hsl[h]], states[h][j]) for j in range(n_sub)]
                outs.append(_dot(a[h], v[:, hsl[h]]) + jnp.concatenate(inter, axis=0))
            out[d] = jnp.concatenate(outs, axis=1)
        yield

    def run_interleaved(gens):
        for _ in zip(*gens):
            pass

    def block_rows(d, j, n):
        return pl.ds(pl.multiple_of((j if d == 0 else n - 1 - j) * rb, rb), rb)

    def ctx_step(j, carry):
        gens = []
        for d, z_ref in ((0, zfc_ref), (1, zbc_ref)):
            rows = block_rows(d, j, n_ctx)
            gens.append(block(d, qc_ref[rows, :], vc_ref[rows, :], z_ref[rows, :], False, None))
        run_interleaved(gens)
        return carry

    def lat_step(j, carry):
        out = {}
        rows = [block_rows(d, j, n_lat) for d in (0, 1)]
        gens = [block(d, ql_ref[rows[d], :], vl_ref[rows[d], :], z_ref[rows[d], :], True, out)
                for d, z_ref in ((0, zfl_ref), (1, zbl_ref))]
        run_interleaved(gens)
        o_ref[rows[0], :] = out[0]
        ob_ref[rows[1], :] = out[1]
        return carry

    st_ref[...] = jnp.zeros_like(st_ref)
    lax.fori_loop(0, n_ctx, ctx_step, 0)
    lax.fori_loop(0, n_lat, lat_step, 0)
    o_ref[...] += ob_ref[...]


def _hgrn2(lat_in, ctx_in, lb_logits, layer, batch, seq, ctx_len):
    w = lat_in[0].shape[1]
    ctx = pl.BlockSpec((ctx_len, HG_LANES), lambda b, g: (b, g))
    lat = pl.BlockSpec((seq, HG_LANES), lambda b, g: (b, g))
    heads = HG_LANES // HG_HEAD_DIM
    return pl.pallas_call(
        functools.partial(_hgrn2_kernel, layer=layer),
        grid=(batch, w // HG_LANES),
        in_specs=[pl.BlockSpec((2, lb_logits.shape[1], HG_LANES), lambda b, g: (0, 0, g))] + [ctx] * 4 + [lat] * 4,
        out_specs=lat,
        out_shape=jax.ShapeDtypeStruct((batch * seq, w), F32),
        scratch_shapes=[pltpu.VMEM((seq, HG_LANES), F32),
                        pltpu.VMEM((2, heads, HG_HEAD_DIM, HG_HEAD_DIM), F32),
                        pltpu.VMEM((2, heads, HG_BLOCK // HG_CHUNK, HG_HEAD_DIM, HG_HEAD_DIM), F32)],
        compiler_params=_params(2),
        name="hgrn2_scan",
    )(lb_logits, *ctx_in, *lat_in)


def _outproj_odd_kernel(x_ref, mod_ref, o_ref, g_ref, gain_ref, w_ref, out_ref, hn_ref):
    for h in range(o_ref.shape[1] // HG_HEAD_DIM):
        sl = slice(h * HG_HEAD_DIM, (h + 1) * HG_HEAD_DIM)
        o = o_ref[:, sl]
        g = g_ref[:, sl].astype(F32)
        ms = jnp.mean(o * o, axis=-1, keepdims=True)
        hn_ref[:, sl] = (o * lax.rsqrt(ms + EPS) * gain_ref[:, sl] * (g * _sigmoid(g))).astype(BF16)
    out_ref[...] = x_ref[...] + mod_ref[0][2:3] * _dot(hn_ref[...], w_ref[...])


def _outproj_odd(xs, mods, o, g, gain, w_out, tm, seq_len):
    rows, w = o.shape
    d = xs.shape[1]
    return pl.pallas_call(
        _outproj_odd_kernel,
        grid=(rows // tm,),
        in_specs=[pl.BlockSpec((tm, d), lambda i: (i, 0)), _mod_spec(d, tm, seq_len, None),
                  pl.BlockSpec((tm, w), lambda i: (i, 0)),
                  pl.BlockSpec((tm, w), lambda i: (i, 0)),
                  _resident((1, w)), _resident(w_out.shape)],
        out_specs=pl.BlockSpec((tm, d), lambda i: (i, 0)),
        out_shape=jax.ShapeDtypeStruct((rows, d), F32),
        scratch_shapes=[pltpu.VMEM((tm, w), BF16)],
        compiler_params=_params(1),
        name="odd_outproj",
    )(xs, mods, o, g, gain.reshape(1, w), w_out)


def kernel(x, c, ctx, c_ctx, w_mod, b_mod, ev_w_in, ev_w_out, da_lq1, da_lk1, da_lq2, da_lk2, da_subln, pool_w,
           pool_scale, hg_w_in, hg_w_out, hg_lb_logits, hg_norm, ffn_w_up, ffn_conv_w, ffn_conv_b, ffn_w_down,
           final_norm):
    batch, seq, d = x.shape
    ctx_len = ctx.shape[1]
    depth = w_mod.shape[0]
    assert seq % TM_LAT == 0 and seq % TQ == 0 and seq % HG_BLOCK == 0
    assert ctx_len % TM_CTX == 0 and ctx_len % HG_BLOCK == 0

    xl = x.reshape(batch * seq, d)
    xc = ctx.reshape(batch * ctx_len, d)
    cond_rows = 2 * SUBLANES
    cond = jnp.zeros((cond_rows, d), F32).at[:batch].set(c).at[batch].set(c_ctx)
    mods_all = _modulation(cond, w_mod, b_mod).reshape(depth, cond_rows, 6, d)
    lat = dict(tm=TM_LAT, seq_len=seq, mod_row=None)
    cxt = dict(tm=TM_CTX, seq_len=ctx_len, mod_row=batch)

    for i in range(depth):
        last = i == depth - 1
        mods = mods_all[i]
        j = i // 2
        ffn_w = (ffn_w_up[i].astype(BF16), ffn_conv_w[i], ffn_conv_b[i], ffn_w_down[i].astype(BF16), final_norm)
        if i % 2 == 0:
            lambda_init = 0.8 - 0.6 * math.exp(-0.3 * i)
            wq = ev_w_in[j].shape[1] // 4
            w_qku = jnp.concatenate([ev_w_in[j][:, :2 * wq], ev_w_in[j][:, 3 * wq:]], axis=1).astype(BF16)
            w_vt = ev_w_in[j][:, 2 * wq:3 * wq].T.astype(BF16)
            w_out = ev_w_out[j].astype(BF16)
            pw = pool_w[j].astype(BF16)
            lqk = jnp.stack([da_lq1[j], da_lk1[j], da_lq2[j], da_lk2[j]]).astype(F32)
            ql, kl, vtl, ul = _inproj_even(xl, mods, w_qku, w_vt, _rope_tables(seq), name="even_inproj_latent", **lat)
            qc, kc, vtc, uc = _inproj_even(xc, mods, w_qku, w_vt, None, name="even_inproj_context", **cxt)
            ol, oc = _diff_attention(ql, kl, vtl, qc, kc, vtc, lqk, da_subln[j], lambda_init, batch, seq, ctx_len)
            xl = _outproj_even(xl, mods, ol, _pool(ul, pw, pool_scale[j], seq, "pool_latent"), w_out,
                               name="even_outproj_latent", **lat)
            if not last:
                xc = _outproj_even(xc, mods, oc, _pool(uc, pw, pool_scale[j], ctx_len, "pool_context"), w_out,
                                   name="even_outproj_context", **cxt)
        else:
            if not last:
                raise NotImplementedError("context readout of an HGRN2 layer that is not the last layer")
            w_in = hg_w_in[j].astype(BF16)
            *lat_in, gl = _inproj_odd(xl, mods, w_in, want_gate=True, name="odd_inproj_latent", **lat)
            ctx_in = _inproj_odd(xc, mods, w_in, want_gate=False, name="odd_inproj_context", **cxt)
            o = _hgrn2(lat_in, ctx_in, hg_lb_logits.astype(F32), i, batch, seq, ctx_len)
            xl = _outproj_odd(xl, mods, o, gl, hg_norm[j], hg_w_out[j].astype(BF16), TM_LAT, seq)
        xl = _conv_ffn(xl, mods, *ffn_w, final=last, name="conv_ffn_latent", **lat)
        if not last:
            xc = _conv_ffn(xc, mods, *ffn_w, final=False, name="conv_ffn_context", **cxt)
    return xl.reshape(batch, seq, d)
```

```python
import functools
import math

import jax
import jax.numpy as jnp
from jax import lax
from jax.experimental import pallas as pl
from jax.experimental.pallas import tpu as pltpu

F32 = jnp.float32
BF16 = jnp.bfloat16

EPS = 1e-6
GRID_W = 64
DA_HEADS = 4
DA_QK_DIM = 64
ROPE_THETA = 10000.0
POOL_WINDOWS = (2, 4, 8, 16)
HG_HEAD_DIM = 128
LANES = 128
SUBLANES = 8
VMEM_LIMIT = 56 * 1024 * 1024

TM_LAT = 512
TM_CTX = 256
TQ = 256
ATTN_KEY_BLOCK = 256
HG_CHUNK = 32
HG_LANES = 256
HG_BLOCK = 256
FF_CHUNK = 1408
LOG2E = math.log2(math.e)


def _resident(shape):
    nd = len(shape)
    return pl.BlockSpec(shape, lambda *_: (0,) * nd, pipeline_mode=pl.Buffered(1))


def _params(n_axes):
    return pltpu.CompilerParams(dimension_semantics=("parallel",) * n_axes, vmem_limit_bytes=VMEM_LIMIT)


def _mod_spec(d, tm, seq_len, fixed_row):
    if fixed_row is None:
        tiles_per_seq = seq_len // tm
        return pl.BlockSpec((1, 6, d), lambda i: (i // tiles_per_seq, 0, 0))
    return pl.BlockSpec((1, 6, d), lambda i: (fixed_row, 0, 0))


def _sigmoid(x):
    return 1.0 / (1.0 + jnp.exp(-x))


def _modnorm(x, shift, scale):
    ms = jnp.mean(x * x, axis=-1, keepdims=True)
    return (x * lax.rsqrt(ms + EPS)) * (1.0 + scale) + shift


def _dot(a, b):
    return jnp.dot(a, b, preferred_element_type=F32)


def _dot_nt(a, b):
    return lax.dot_general(a, b, (((1,), (1,)), ((), ())), preferred_element_type=F32)


def _dot_tn(a, b):
    return lax.dot_general(a, b, (((0,), (0,)), ((), ())), preferred_element_type=F32)


def _mod_kernel(cond_ref, w_ref, b_ref, out_ref):
    c = cond_ref[...]
    s = c * _sigmoid(c)
    out_ref[0] = jnp.dot(s, w_ref[0], precision=lax.Precision.HIGHEST,
                         preferred_element_type=F32) + b_ref[0]


def _modulation(cond, w_mod, b_mod):
    depth, d, n = w_mod.shape
    rows = cond.shape[0]
    tn = 1024
    return pl.pallas_call(
        _mod_kernel,
        grid=(depth, n // tn),
        in_specs=[pl.BlockSpec((rows, d), lambda l, j: (0, 0)),
                  pl.BlockSpec((1, d, tn), lambda l, j: (l, 0, j)),
                  pl.BlockSpec((1, 1, tn), lambda l, j: (l, 0, j))],
        out_specs=pl.BlockSpec((1, rows, tn), lambda l, j: (l, 0, j)),
        out_shape=jax.ShapeDtypeStruct((depth, rows, n), F32),
        compiler_params=_params(2),
        name="adaln_mod",
    )(cond, w_mod, b_mod.reshape(depth, 1, n))


def _inproj_even_kernel(*refs, has_rope):
    if has_rope:
        x_ref, mod_ref, w_ref, wvt_ref, cos_ref, sa_ref, sb_ref, q_ref, k_ref, vt_ref, u_ref = refs
    else:
        x_ref, mod_ref, w_ref, wvt_ref, q_ref, k_ref, vt_ref, u_ref = refs
    m = mod_ref[0]
    h = _modnorm(x_ref[...], m[0:1], m[1:2]).astype(BF16)
    w = q_ref.shape[1]
    pq = _dot(h, w_ref[:, 0:w])
    pk = _dot(h, w_ref[:, w:2 * w])
    if has_rope:
        reps = w // LANES
        cos = jnp.concatenate([cos_ref[...]] * reps, axis=1)
        sa = jnp.concatenate([sa_ref[...]] * reps, axis=1)
        sb = jnp.concatenate([sb_ref[...]] * reps, axis=1)
        half = DA_QK_DIM // 4

        def rope(p):
            return p * cos + pltpu.roll(p, w - half, axis=1) * sa + pltpu.roll(p, half, axis=1) * sb

        pq, pk = rope(pq), rope(pk)
    q_ref[...] = (pq * (DA_QK_DIM ** -0.5 * LOG2E)).astype(BF16)
    k_ref[...] = pk.astype(BF16)
    vt_ref[...] = _dot_nt(wvt_ref[...], h).astype(BF16)
    u_ref[...] = _dot(h, w_ref[:, 2 * w:3 * w])


def _inproj_even(xs, mods, w_qku, w_vt, rope_tabs, tm, seq_len, mod_row, name):
    rows, d = xs.shape
    w = w_vt.shape[0]
    has_rope = rope_tabs is not None
    in_specs = [pl.BlockSpec((tm, d), lambda i: (i, 0)), _mod_spec(d, tm, seq_len, mod_row),
                _resident(w_qku.shape), _resident(w_vt.shape)]
    args = [xs, mods, w_qku, w_vt]
    if has_rope:
        tiles_per_seq = seq_len // tm
        in_specs += [pl.BlockSpec((tm, LANES), lambda i: (i % tiles_per_seq, 0))] * 3
        args += list(rope_tabs)
    tok = lambda dt: jax.ShapeDtypeStruct((rows, w), dt)
    tok_spec = pl.BlockSpec((tm, w), lambda i: (i, 0))
    return pl.pallas_call(
        functools.partial(_inproj_even_kernel, has_rope=has_rope),
        grid=(rows // tm,),
        in_specs=in_specs,
        out_specs=[tok_spec, tok_spec, pl.BlockSpec((w, tm), lambda i: (0, i)), tok_spec],
        out_shape=[tok(BF16), tok(BF16), jax.ShapeDtypeStruct((w, rows), BF16), tok(F32)],
        compiler_params=_params(1),
        name=name,
    )(*args)


def _rope_tables(seq):
    rows_n = seq // GRID_W
    row = jnp.repeat(jnp.arange(rows_n, dtype=F32), GRID_W)
    col = jnp.tile(jnp.arange(GRID_W, dtype=F32), rows_n)
    n_freq = DA_QK_DIM // 4
    inv = ROPE_THETA ** (-jnp.arange(n_freq, dtype=F32) / n_freq)
    ang_r, ang_c = row[:, None] * inv, col[:, None] * inv
    zeros = jnp.zeros_like(ang_r)
    cos64 = jnp.concatenate([jnp.cos(ang_r)] * 2 + [jnp.cos(ang_c)] * 2, axis=1)
    sa64 = jnp.concatenate([-jnp.sin(ang_r), zeros, -jnp.sin(ang_c), zeros], axis=1)
    sb64 = jnp.concatenate([zeros, jnp.sin(ang_r), zeros, jnp.sin(ang_c)], axis=1)
    return tuple(jnp.concatenate([t, t], axis=1) for t in (cos64, sa64, sb64))


def _attn_kernel(*refs, lambda_init, has_lat):
    if has_lat:
        lqk_ref, gain_ref, q_ref, kc_ref, vtc_ref, kl_ref, vtl_ref, o_ref, s_ref = refs
    else:
        lqk_ref, gain_ref, q_ref, kc_ref, vtc_ref, o_ref, s_ref = refs
    tq = s_ref.shape[3]
    nq = q_ref.shape[0] // tq
    n_ctx = kc_ref.shape[0]
    a = lqk_ref[...]
    lam = (jnp.exp(jnp.sum(a[0:1] * a[1:2], axis=-1, keepdims=True))
           - jnp.exp(jnp.sum(a[2:3] * a[3:4], axis=-1, keepdims=True)) + lambda_init)
    lane = lax.broadcasted_iota(jnp.int32, (1, LANES), 1)
    map_lanes = [lane < DA_QK_DIM, lane >= DA_QK_DIM]
    gain = gain_ref[...] * (1.0 - lambda_init)

    def tile_rows(t):
        return pl.ds(pl.multiple_of(t * tq, tq), tq)

    def scores(t, slot):
        q = q_ref[tile_rows(t), :]
        for i, sel in enumerate(map_lanes):
            qm = jnp.where(sel, q, jnp.zeros_like(q))
            s_ref[slot, i, 0:n_ctx, :] = _dot_nt(kc_ref[...], qm)
            if has_lat:
                s_ref[slot, i, n_ctx:, :] = _dot_nt(kl_ref[...], qm)

    def softmax_pv(t, slot):
        outs = []
        for i in range(2):
            s = s_ref[slot, i]
            e = jnp.exp2(s - jnp.max(s, axis=0, keepdims=True))
            l = jnp.sum(e, axis=0, keepdims=True)
            e = e.astype(BF16)
            o = _dot(vtc_ref[...], e[0:n_ctx])
            if has_lat:
                o = o + _dot(vtl_ref[...], e[n_ctx:])
            outs.append(o / l)
        ot = outs[0] - lam * outs[1]
        ms = jnp.mean(ot * ot, axis=0, keepdims=True)
        o_ref[tile_rows(t), :] = ((ot * lax.rsqrt(ms + EPS)).T * gain).astype(BF16)

    scores(0, 0)
    if nq > 1:
        def tile_pair(i, carry):
            t = 2 * i
            scores(t + 1, 1)
            softmax_pv(t, 0)
            scores(t + 2, 0)
            softmax_pv(t + 1, 1)
            return carry

        lax.fori_loop(0, nq // 2 - 1, tile_pair, 0)
        scores(nq - 1, 1)
        softmax_pv(nq - 2, 0)
        softmax_pv(nq - 1, 1)
    else:
        softmax_pv(0, 0)


def _diff_attention(q_lat, k_lat, vt_lat, q_ctx, k_ctx, vt_ctx, lqk, subln, lambda_init, batch, seq, ctx_len):
    w = q_lat.shape[1]
    assert (seq // TQ) % 2 == 0
    gain = subln.reshape(1, w)
    small = [pl.BlockSpec(lqk.shape, lambda b, h: (0, 0)),
             pl.BlockSpec((1, LANES), lambda b, h: (0, h))]
    ctx_k = pl.BlockSpec((ctx_len, LANES), lambda b, h: (b, h))
    ctx_vt = pl.BlockSpec((LANES, ctx_len), lambda b, h: (h, b))
    lat_k = pl.BlockSpec((seq, LANES), lambda b, h: (b, h))
    lat_vt = pl.BlockSpec((LANES, seq), lambda b, h: (h, b))
    o_lat = pl.pallas_call(
        functools.partial(_attn_kernel, lambda_init=lambda_init, has_lat=True),
        grid=(batch, DA_HEADS),
        in_specs=small + [lat_k, ctx_k, ctx_vt, lat_k, lat_vt],
        out_specs=lat_k,
        out_shape=jax.ShapeDtypeStruct(q_lat.shape, BF16),
        scratch_shapes=[pltpu.VMEM((2, 2, ctx_len + seq, TQ), F32)],
        compiler_params=_params(2),
        name="diff_attn_latent",
    )(lqk, gain, q_lat, k_ctx, vt_ctx, k_lat, vt_lat)
    o_ctx = pl.pallas_call(
        functools.partial(_attn_kernel, lambda_init=lambda_init, has_lat=False),
        grid=(batch, DA_HEADS),
        in_specs=small + [ctx_k, ctx_k, ctx_vt],
        out_specs=ctx_k,
        out_shape=jax.ShapeDtypeStruct(q_ctx.shape, BF16),
        scratch_shapes=[pltpu.VMEM((1, 2, ctx_len, ctx_len), F32)],
        compiler_params=_params(2),
        name="diff_attn_context",
    )(lqk, gain, q_ctx, k_ctx, vt_ctx)
    return o_lat, o_ctx


def _pool_kernel(u_ref, pw_ref, ps_ref, out_ref):
    n = u_ref.shape[0]
    t = lax.broadcasted_iota(jnp.int32, (n, 1), 0)
    for g, w in enumerate(POOL_WINDOWS):
        sl = slice(g * LANES, (g + 1) * LANES)
        ug = u_ref[:, sl]
        lo, hi = -(w // 2), w - w // 2
        acc = ug
        for off in range(lo, hi):
            if off == 0:
                continue
            valid = (t + off >= 0) & (t + off < n)
            acc = acc + jnp.where(valid, pltpu.roll(ug, (-off) % n, axis=0), 0.0)
        cnt = (jnp.minimum(t + hi, n) - jnp.maximum(t + lo, 0)).astype(F32)
        dlt = acc / cnt - ug
        y = _dot(dlt.astype(BF16), pw_ref[g]) * ps_ref[:, sl]
        out_ref[:, sl] = y.astype(BF16)


def _pool(u, pool_w, pool_scale, seq_len, name):
    rows, w = u.shape
    return pl.pallas_call(
        _pool_kernel,
        grid=(rows // seq_len,),
        in_specs=[pl.BlockSpec((seq_len, w), lambda b: (b, 0)),
                  pl.BlockSpec(pool_w.shape, lambda b: (0, 0, 0)),
                  pl.BlockSpec((1, w), lambda b: (0, 0))],
        out_specs=pl.BlockSpec((seq_len, w), lambda b: (b, 0)),
        out_shape=jax.ShapeDtypeStruct((rows, w), BF16),
        compiler_params=_params(1),
        name=name,
    )(u, pool_w, pool_scale.reshape(1, w))


def _outproj_even_kernel(x_ref, mod_ref, o_ref, p_ref, w_ref, out_ref):
    k = o_ref.shape[1]
    y = _dot(o_ref[...], w_ref[0:k, :]) + _dot(p_ref[...], w_ref[k:, :])
    out_ref[...] = x_ref[...] + mod_ref[0][2:3] * y


def _outproj_even(xs, mods, o, p, w_out, tm, seq_len, mod_row, name):
    rows, d = xs.shape
    k = o.shape[1]
    return pl.pallas_call(
        _outproj_even_kernel,
        grid=(rows // tm,),
        in_specs=[pl.BlockSpec((tm, d), lambda i: (i, 0)), _mod_spec(d, tm, seq_len, mod_row),
                  pl.BlockSpec((tm, k), lambda i: (i, 0)),
                  pl.BlockSpec((tm, k), lambda i: (i, 0)),
                  _resident(w_out.shape)],
        out_specs=pl.BlockSpec((tm, d), lambda i: (i, 0)),
        out_shape=jax.ShapeDtypeStruct((rows, d), F32),
        compiler_params=_params(1),
        name=name,
    )(xs, mods, o, p, w_out)


def _ffn_kernel(x_ref, xp_ref, xn_ref, mod_ref, wu_ref, cw_ref, cb_ref, wd_ref, fin_ref, out_ref,
                *, tiles_per_seq, final):
    i = pl.program_id(0)
    tm = x_ref.shape[0]
    d_ff = wd_ref.shape[0]
    at_start = i % tiles_per_seq == 0
    at_end = (i + 1) % tiles_per_seq == 0
    m = mod_ref[0]
    shift, scale = m[3:4], m[4:5]
    x = x_ref[...]
    hp = jnp.where(at_start, 0.0, _modnorm(xp_ref[...], shift, scale))
    hn = jnp.where(at_end, 0.0, _modnorm(xn_ref[...], shift, scale))
    h = jnp.concatenate([hp, _modnorm(x, shift, scale), hn], axis=0).astype(BF16)
    ext = tm + 2 * SUBLANES

    def conv(u, c0):
        cw = cw_ref[:, c0:c0 + FF_CHUNK]
        y = (pltpu.roll(u, 1, axis=0) * cw[0:1] + u * cw[1:2] + pltpu.roll(u, ext - 1, axis=0) * cw[2:3]
             + cb_ref[:, c0:c0 + FF_CHUNK])
        return y[SUBLANES:SUBLANES + tm]

    acc = jnp.zeros((tm, out_ref.shape[1]), F32)
    for j in range(d_ff // FF_CHUNK):
        ca, cg = j * FF_CHUNK, d_ff + j * FF_CHUNK
        a = conv(_dot(h, wu_ref[:, ca:ca + FF_CHUNK]), ca)
        g = conv(_dot(h, wu_ref[:, cg:cg + FF_CHUNK]), cg)
        act = (a * (g * _sigmoid(g))).astype(BF16)
        acc = acc + _dot(act, wd_ref[ca:ca + FF_CHUNK, :])
    y = x + m[5:6] * acc
    if final:
        ms = jnp.mean(y * y, axis=-1, keepdims=True)
        y = y * lax.rsqrt(ms + EPS) * fin_ref[...]
    out_ref[...] = y


def _conv_ffn(xs, mods, w_up, conv_w, conv_b, w_down, final_norm, tm, seq_len, mod_row, final, name):
    rows, d = xs.shape
    tb = tm // SUBLANES
    last = rows // SUBLANES - 1
    return pl.pallas_call(
        functools.partial(_ffn_kernel, tiles_per_seq=seq_len // tm, final=final),
        grid=(rows // tm,),
        in_specs=[pl.BlockSpec((tm, d), lambda i: (i, 0)),
                  pl.BlockSpec((SUBLANES, d), lambda i: (jnp.maximum(i * tb - 1, 0), 0)),
                  pl.BlockSpec((SUBLANES, d), lambda i: (jnp.minimum((i + 1) * tb, last), 0)),
                  _mod_spec(d, tm, seq_len, mod_row),
                  _resident(w_up.shape), _resident(conv_w.shape), _resident((1, conv_b.shape[0])),
                  _resident(w_down.shape), _resident((1, d))],
        out_specs=pl.BlockSpec((tm, d), lambda i: (i, 0)),
        out_shape=jax.ShapeDtypeStruct((rows, d), F32),
        compiler_params=_params(1),
        name=name,
    )(xs, xs, xs, mods, w_up, conv_w, conv_b.reshape(1, -1), w_down, final_norm.reshape(1, d))


def _inproj_odd_kernel(x_ref, mod_ref, w_ref, *out_refs):
    m = mod_ref[0]
    h = _modnorm(x_ref[...], m[0:1], m[1:2]).astype(BF16)
    w = out_refs[0].shape[1]
    step = 512
    for c0 in range(0, w, step):
        sl = slice(c0, c0 + step)
        for n, ref in enumerate(out_refs):
            p = _dot(h, w_ref[:, n * w + c0:n * w + c0 + step])
            if n == 0:
                p = p * _sigmoid(p)
            ref[:, sl] = p.astype(ref.dtype)


def _inproj_odd(xs, mods, w_in, tm, seq_len, mod_row, want_gate, name):
    rows, d = xs.shape
    w = w_in.shape[1] // 5
    dtypes = [BF16, BF16, F32, F32] + ([BF16] if want_gate else [])
    return pl.pallas_call(
        _inproj_odd_kernel,
        grid=(rows // tm,),
        in_specs=[pl.BlockSpec((tm, d), lambda i: (i, 0)), _mod_spec(d, tm, seq_len, mod_row),
                  _resident(w_in.shape)],
        out_specs=[pl.BlockSpec((tm, w), lambda i: (i, 0))] * len(dtypes),
        out_shape=[jax.ShapeDtypeStruct((rows, w), dt) for dt in dtypes],
        compiler_params=_params(1),
        name=name,
    )(xs, mods, w_in)


def _cumsum_rows(tri, x):
    w = x.shape[1]
    hi = x.astype(BF16)
    r1 = x - hi.astype(F32)
    mid = r1.astype(BF16)
    lo = (r1 - mid.astype(F32)).astype(BF16)
    s = _dot(tri, jnp.concatenate([hi, mid, lo], axis=1))
    return s[:, 0:w] + s[:, w:2 * w] + s[:, 2 * w:3 * w]


def _hgrn2_kernel(lbl_ref, qc_ref, vc_ref, zfc_ref, zbc_ref, ql_ref, vl_ref, zfl_ref, zbl_ref,
                  o_ref, ob_ref, st_ref, *, layer):
    c = HG_CHUNK
    rb = HG_BLOCK
    n_sub = rb // c
    n_levels = n_sub.bit_length() - 1
    w = o_ref.shape[1]
    heads = w // HG_HEAD_DIM
    n_ctx = qc_ref.shape[0] // rb
    n_lat = ql_ref.shape[0] // rb
    r_i = lax.broadcasted_iota(jnp.int32, (rb, rb), 0)
    c_i = lax.broadcasted_iota(jnp.int32, (rb, rb), 1)
    same_group = [(r_i // (c << l)) == (c_i // (c << l)) for l in range(n_levels)]
    chunk_id = lax.broadcasted_iota(jnp.int32, (n_sub, 1, w), 0)
    hsl = [slice(h * HG_HEAD_DIM, (h + 1) * HG_HEAD_DIM) for h in range(heads)]

    def direction_constants(d):
        fwd = d == 0
        lg = lbl_ref[d]
        e = jnp.exp(lg - jnp.max(lg, axis=0, keepdims=True))
        p = e / jnp.sum(e, axis=0, keepdims=True)
        lb = jnp.sum(p[1:layer + 1], axis=0, keepdims=True)
        ordered = (c_i <= r_i) if fwd else (c_i >= r_i)
        tri = jnp.where(ordered, 1.0, 0.0).astype(BF16)
        return lb, same_group[0] & ordered, tri

    consts = [direction_constants(0), direction_constants(1)]

    def block(d, q, v, z, want_out, out):
        fwd = d == 0
        lb, causal, tri = consts[d]
        split = lambda a, n: a.reshape(rb // n, n, w)
        merge = lambda a: a.reshape(rb, w).astype(BF16)
        f = lb + (1.0 - lb) * _sigmoid(z)
        k = 1.0 - f
        logf = jnp.log(f)
        yield
        bg = _cumsum_rows(tri, logf)
        yield
        be = bg[rb - 1:rb] if fwd else bg[0:1]
        decay = jnp.exp(be)
        if want_out:
            b0 = split(bg, c)
            r0 = b0[:, c // 2 - 1:c // 2, :] if fwd else b0[:, c // 2:c // 2 + 1, :]
            qt = split(q.astype(F32), c) * jnp.exp(b0 - r0)
            kt = split(k, c) * jnp.exp(r0 - b0)
            q_lv, k_lv = [merge(qt)], [merge(kt)]
            for l in range(1, n_levels + 1):
                g = c << l
                bl = split(bg, g)
                m = bl[:, g // 2 - 1:g // 2, :] if fwd else bl[:, g // 2:g // 2 + 1, :]
                m = jnp.broadcast_to(m[:, None], (rb // g, 1 << l, 1, w)).reshape(n_sub, 1, w)
                late_half = ((chunk_id >> (l - 1)) & 1) == 1
                q_side = late_half if fwd else ~late_half
                q_lv.append(merge(qt * jnp.where(q_side, jnp.exp(r0 - m), 0.0)))
                k_lv.append(merge(kt * jnp.where(q_side, 0.0, jnp.exp(m - r0))))
            qh = merge(qt * jnp.exp(r0))
            kh = merge(kt * jnp.exp(be - r0))
        else:
            kh = (k * jnp.exp(be - bg)).astype(BF16)
        yield
        ds = [_dot_tn(v[:, sl], kh[:, sl]) for sl in hsl]
        if want_out:
            a_lv = [[_dot_nt(ql[:, sl], kl[:, sl]) for ql, kl in zip(q_lv, k_lv)] for sl in hsl]
        yield
        st_old = []
        for h in range(heads):
            st = st_ref[d, h]
            st_old.append(st.astype(BF16))
            st_ref[d, h] = st * decay[:, hsl[h]] + ds[h]
        if want_out:
            a = []
            for lv in a_lv:
                x = lv[n_levels]
                for l in range(n_levels - 1, 0, -1):
                    x = jnp.where(same_group[l], lv[l], x)
                a.append(jnp.where(causal, lv[0], x).astype(BF16))
        yield
        if want_out:
            out[d] = jnp.concatenate([_dot(a[h], v[:, hsl[h]]) + _dot_nt(qh[:, hsl[h]], st_old[h])
                                      for h in range(heads)], axis=1)
        yield

    def run_interleaved(gens):
        for _ in zip(*gens):
            pass

    def block_rows(d, j, n):
        return pl.ds(pl.multiple_of((j if d == 0 else n - 1 - j) * rb, rb), rb)

    def ctx_step(j, carry):
        gens = []
        for d, z_ref in ((0, zfc_ref), (1, zbc_ref)):
            rows = block_rows(d, j, n_ctx)
            gens.append(block(d, qc_ref[rows, :], vc_ref[rows, :], z_ref[rows, :], False, None))
        run_interleaved(gens)
        return carry

    def lat_step(j, carry):
        out = {}
        rows = [block_rows(d, j, n_lat) for d in (0, 1)]
        gens = [block(d, ql_ref[rows[d], :], vl_ref[rows[d], :], z_ref[rows[d], :], True, out)
                for d, z_ref in ((0, zfl_ref), (1, zbl_ref))]
        run_interleaved(gens)
        o_ref[rows[0], :] = out[0]
        ob_ref[rows[1], :] = out[1]
        return carry

    st_ref[...] = jnp.zeros_like(st_ref)
    lax.fori_loop(0, n_ctx, ctx_step, 0)
    lax.fori_loop(0, n_lat, lat_step, 0)
    o_ref[...] += ob_ref[...]


def _hgrn2(lat_in, ctx_in, lb_logits, layer, batch, seq, ctx_len):
    w = lat_in[0].shape[1]
    ctx = pl.BlockSpec((ctx_len, HG_LANES), lambda b, g: (b, g))
    lat = pl.BlockSpec((seq, HG_LANES), lambda b, g: (b, g))
    heads = HG_LANES // HG_HEAD_DIM
    return pl.pallas_call(
        functools.partial(_hgrn2_kernel, layer=layer),
        grid=(batch, w // HG_LANES),
        in_specs=[pl.BlockSpec((2, lb_logits.shape[1], HG_LANES), lambda b, g: (0, 0, g))] + [ctx] * 4 + [lat] * 4,
        out_specs=lat,
        out_shape=jax.ShapeDtypeStruct((batch * seq, w), F32),
        scratch_shapes=[pltpu.VMEM((seq, HG_LANES), F32),
                        pltpu.VMEM((2, heads, HG_HEAD_DIM, HG_HEAD_DIM), F32)],
        compiler_params=_params(2),
        name="hgrn2_scan",
    )(lb_logits, *ctx_in, *lat_in)


def _outproj_odd_kernel(x_ref, mod_ref, o_ref, g_ref, gain_ref, w_ref, out_ref, hn_ref):
    for h in range(o_ref.shape[1] // HG_HEAD_DIM):
        sl = slice(h * HG_HEAD_DIM, (h + 1) * HG_HEAD_DIM)
        o = o_ref[:, sl]
        g = g_ref[:, sl].astype(F32)
        ms = jnp.mean(o * o, axis=-1, keepdims=True)
        hn_ref[:, sl] = (o * lax.rsqrt(ms + EPS) * gain_ref[:, sl] * (g * _sigmoid(g))).astype(BF16)
    out_ref[...] = x_ref[...] + mod_ref[0][2:3] * _dot(hn_ref[...], w_ref[...])


def _outproj_odd(xs, mods, o, g, gain, w_out, tm, seq_len):
    rows, w = o.shape
    d = xs.shape[1]
    return pl.pallas_call(
        _outproj_odd_kernel,
        grid=(rows // tm,),
        in_specs=[pl.BlockSpec((tm, d), lambda i: (i, 0)), _mod_spec(d, tm, seq_len, None),
                  pl.BlockSpec((tm, w), lambda i: (i, 0)),
                  pl.BlockSpec((tm, w), lambda i: (i, 0)),
                  _resident((1, w)), _resident(w_out.shape)],
        out_specs=pl.BlockSpec((tm, d), lambda i: (i, 0)),
        out_shape=jax.ShapeDtypeStruct((rows, d), F32),
        scratch_shapes=[pltpu.VMEM((tm, w), BF16)],
        compiler_params=_params(1),
        name="odd_outproj",
    )(xs, mods, o, g, gain.reshape(1, w), w_out)


def kernel(x, c, ctx, c_ctx, w_mod, b_mod, ev_w_in, ev_w_out, da_lq1, da_lk1, da_lq2, da_lk2, da_subln, pool_w,
           pool_scale, hg_w_in, hg_w_out, hg_lb_logits, hg_norm, ffn_w_up, ffn_conv_w, ffn_conv_b, ffn_w_down,
           final_norm):
    batch, seq, d = x.shape
    ctx_len = ctx.shape[1]
    depth = w_mod.shape[0]
    assert seq % TM_LAT == 0 and seq % TQ == 0 and seq % HG_BLOCK == 0
    assert ctx_len % TM_CTX == 0 and ctx_len % HG_BLOCK == 0

    xl = x.reshape(batch * seq, d)
    xc = ctx.reshape(batch * ctx_len, d)
    cond_rows = 2 * SUBLANES
    cond = jnp.zeros((cond_rows, d), F32).at[:batch].set(c).at[batch].set(c_ctx)
    mods_all = _modulation(cond, w_mod, b_mod).reshape(depth, cond_rows, 6, d)
    lat = dict(tm=TM_LAT, seq_len=seq, mod_row=None)
    cxt = dict(tm=TM_CTX, seq_len=ctx_len, mod_row=batch)

    for i in range(depth):
        last = i == depth - 1
        mods = mods_all[i]
        j = i // 2
        ffn_w = (ffn_w_up[i].astype(BF16), ffn_conv_w[i], ffn_conv_b[i], ffn_w_down[i].astype(BF16), final_norm)
        if i % 2 == 0:
            lambda_init = 0.8 - 0.6 * math.exp(-0.3 * i)
            wq = ev_w_in[j].shape[1] // 4
            w_qku = jnp.concatenate([ev_w_in[j][:, :2 * wq], ev_w_in[j][:, 3 * wq:]], axis=1).astype(BF16)
            w_vt = ev_w_in[j][:, 2 * wq:3 * wq].T.astype(BF16)
            w_out = ev_w_out[j].astype(BF16)
            pw = pool_w[j].astype(BF16)
            lqk = jnp.stack([da_lq1[j], da_lk1[j], da_lq2[j], da_lk2[j]]).astype(F32)
            ql, kl, vtl, ul = _inproj_even(xl, mods, w_qku, w_vt, _rope_tables(seq), name="even_inproj_latent", **lat)
            qc, kc, vtc, uc = _inproj_even(xc, mods, w_qku, w_vt, None, name="even_inproj_context", **cxt)
            ol, oc = _diff_attention(ql, kl, vtl, qc, kc, vtc, lqk, da_subln[j], lambda_init, batch, seq, ctx_len)
            xl = _outproj_even(xl, mods, ol, _pool(ul, pw, pool_scale[j], seq, "pool_latent"), w_out,
                               name="even_outproj_latent", **lat)
            if not last:
                xc = _outproj_even(xc, mods, oc, _pool(uc, pw, pool_scale[j], ctx_len, "pool_context"), w_out,
                                   name="even_outproj_context", **cxt)
        else:
            if not last:
                raise NotImplementedError("context readout of an HGRN2 layer that is not the last layer")
            w_in = hg_w_in[j].astype(BF16)
            *lat_in, gl = _inproj_odd(xl, mods, w_in, want_gate=True, name="odd_inproj_latent", **lat)
            ctx_in = _inproj_odd(xc, mods, w_in, want_gate=False, name="odd_inproj_context", **cxt)
            o = _hgrn2(lat_in, ctx_in, hg_lb_logits.astype(F32), i, batch, seq, ctx_len)
            xl = _outproj_odd(xl, mods, o, gl, hg_norm[j], hg_w_out[j].astype(BF16), TM_LAT, seq)
        xl = _conv_ffn(xl, mods, *ffn_w, final=last, name="conv_ffn_latent", **lat)
        if not last:
            xc = _conv_ffn(xc, mods, *ffn_w, final=False, name="conv_ffn_context", **cxt)
    return xl.reshape(batch, seq, d)
```

```python
import functools
import math

import jax
import jax.numpy as jnp
import numpy as np
from jax import lax
from jax.experimental import pallas as pl
from jax.experimental.pallas import tpu as pltpu

F32 = jnp.float32
BF16 = jnp.bfloat16

EPS = 1e-6
GRID_W = 64
DA_HEADS = 4
DA_QK_DIM = 64
ROPE_THETA = 10000.0
POOL_WINDOWS = (2, 4, 8, 16)
HG_HEAD_DIM = 128
LANES = 128
SUBLANES = 8
VMEM_LIMIT = 56 * 1024 * 1024

TM_LAT = 512
TM_CTX = 256
TQ = 256
ATTN_KEY_BLOCK = 256
HG_CHUNK = 32
HG_LANES = 256
HG_BLOCK = 256
MXU_DIM = 256
FF_CHUNK = 1536
LOG2E = math.log2(math.e)


def _resident(shape):
    nd = len(shape)
    return pl.BlockSpec(shape, lambda *_: (0,) * nd, pipeline_mode=pl.Buffered(1))


def _params(n_axes):
    return pltpu.CompilerParams(dimension_semantics=("parallel",) * n_axes, vmem_limit_bytes=VMEM_LIMIT)


def _mod_spec(d, tm, seq_len, fixed_row):
    if fixed_row is None:
        tiles_per_seq = seq_len // tm
        return pl.BlockSpec((1, 6, d), lambda i: (i // tiles_per_seq, 0, 0))
    return pl.BlockSpec((1, 6, d), lambda i: (fixed_row, 0, 0))


def _sigmoid(x):
    return 1.0 / (1.0 + jnp.exp(-x))


def _modnorm(x, shift, scale):
    ms = jnp.mean(x * x, axis=-1, keepdims=True)
    return (x * lax.rsqrt(ms + EPS)) * (1.0 + scale) + shift


def _dot(a, b):
    return jnp.dot(a, b, preferred_element_type=F32)


def _dot_nt(a, b):
    return lax.dot_general(a, b, (((1,), (1,)), ((), ())), preferred_element_type=F32)


def _dot_tn(a, b):
    return lax.dot_general(a, b, (((0,), (0,)), ((), ())), preferred_element_type=F32)


def _mod_kernel(cond_ref, w_ref, b_ref, out_ref):
    c = cond_ref[...]
    s = c * _sigmoid(c)
    out_ref[0] = jnp.dot(s, w_ref[0], precision=lax.Precision.HIGHEST,
                         preferred_element_type=F32) + b_ref[0]


def _modulation(cond, w_mod, b_mod):
    depth, d, n = w_mod.shape
    rows = cond.shape[0]
    tn = 1024
    return pl.pallas_call(
        _mod_kernel,
        grid=(depth, n // tn),
        in_specs=[pl.BlockSpec((rows, d), lambda l, j: (0, 0)),
                  pl.BlockSpec((1, d, tn), lambda l, j: (l, 0, j)),
                  pl.BlockSpec((1, 1, tn), lambda l, j: (l, 0, j))],
        out_specs=pl.BlockSpec((1, rows, tn), lambda l, j: (l, 0, j)),
        out_shape=jax.ShapeDtypeStruct((depth, rows, n), F32),
        compiler_params=_params(2),
        name="adaln_mod",
    )(cond, w_mod, b_mod.reshape(depth, 1, n))


def _inproj_even_kernel(*refs, has_rope):
    if has_rope:
        x_ref, mod_ref, w_ref, wvt_ref, cos_ref, sa_ref, sb_ref, q_ref, k_ref, vt_ref, u_ref = refs
    else:
        x_ref, mod_ref, w_ref, wvt_ref, q_ref, k_ref, vt_ref, u_ref = refs
    m = mod_ref[0]
    h = _modnorm(x_ref[...], m[0:1], m[1:2]).astype(BF16)
    w = q_ref.shape[1]
    pq = _dot(h, w_ref[:, 0:w])
    pk = _dot(h, w_ref[:, w:2 * w])
    if has_rope:
        reps = w // LANES
        cos = jnp.concatenate([cos_ref[...]] * reps, axis=1)
        sa = jnp.concatenate([sa_ref[...]] * reps, axis=1)
        sb = jnp.concatenate([sb_ref[...]] * reps, axis=1)
        half = DA_QK_DIM // 4

        def rope(p):
            return p * cos + pltpu.roll(p, w - half, axis=1) * sa + pltpu.roll(p, half, axis=1) * sb

        pq, pk = rope(pq), rope(pk)
    q_ref[...] = (pq * (DA_QK_DIM ** -0.5 * LOG2E)).astype(BF16)
    k_ref[...] = pk.astype(BF16)
    vt_ref[...] = _dot_nt(wvt_ref[...], h).astype(BF16)
    u_ref[...] = _dot(h, w_ref[:, 2 * w:3 * w])


def _inproj_even(xs, mods, w_qku, w_vt, rope_tabs, tm, seq_len, mod_row, name):
    rows, d = xs.shape
    w = w_vt.shape[0]
    has_rope = rope_tabs is not None
    in_specs = [pl.BlockSpec((tm, d), lambda i: (i, 0)), _mod_spec(d, tm, seq_len, mod_row),
                _resident(w_qku.shape), _resident(w_vt.shape)]
    args = [xs, mods, w_qku, w_vt]
    if has_rope:
        tiles_per_seq = seq_len // tm
        in_specs += [pl.BlockSpec((tm, LANES), lambda i: (i % tiles_per_seq, 0))] * 3
        args += list(rope_tabs)
    tok = lambda dt: jax.ShapeDtypeStruct((rows, w), dt)
    tok_spec = pl.BlockSpec((tm, w), lambda i: (i, 0))
    return pl.pallas_call(
        functools.partial(_inproj_even_kernel, has_rope=has_rope),
        grid=(rows // tm,),
        in_specs=in_specs,
        out_specs=[tok_spec, tok_spec, pl.BlockSpec((w, tm), lambda i: (0, i)), tok_spec],
        out_shape=[tok(BF16), tok(BF16), jax.ShapeDtypeStruct((w, rows), BF16), tok(F32)],
        compiler_params=_params(1),
        name=name,
    )(*args)


def _rope_tables(seq):
    rows_n = seq // GRID_W
    row = np.repeat(np.arange(rows_n, dtype=np.float64), GRID_W)
    col = np.tile(np.arange(GRID_W, dtype=np.float64), rows_n)
    n_freq = DA_QK_DIM // 4
    inv = ROPE_THETA ** (-np.arange(n_freq, dtype=np.float64) / n_freq)
    ang_r, ang_c = row[:, None] * inv, col[:, None] * inv
    zeros = np.zeros_like(ang_r)
    cos64 = np.concatenate([np.cos(ang_r)] * 2 + [np.cos(ang_c)] * 2, axis=1)
    sa64 = np.concatenate([-np.sin(ang_r), zeros, -np.sin(ang_c), zeros], axis=1)
    sb64 = np.concatenate([zeros, np.sin(ang_r), zeros, np.sin(ang_c)], axis=1)
    return tuple(jnp.asarray(np.concatenate([t, t], axis=1), dtype=F32) for t in (cos64, sa64, sb64))


def _attn_kernel(*refs, lambda_init, has_lat):
    if has_lat:
        lqk_ref, gain_ref, q_ref, kc_ref, vtc_ref, kl_ref, vtl_ref, o_ref, s_ref = refs
    else:
        lqk_ref, gain_ref, q_ref, kc_ref, vtc_ref, o_ref, s_ref = refs
    tq = s_ref.shape[3]
    nq = q_ref.shape[0] // tq
    n_ctx = kc_ref.shape[0]
    a = lqk_ref[...]
    lam = (jnp.exp(jnp.sum(a[0:1] * a[1:2], axis=-1, keepdims=True))
           - jnp.exp(jnp.sum(a[2:3] * a[3:4], axis=-1, keepdims=True)) + lambda_init)
    lane = lax.broadcasted_iota(jnp.int32, (1, LANES), 1)
    map_lanes = [lane < DA_QK_DIM, lane >= DA_QK_DIM]
    gain = gain_ref[...] * (1.0 - lambda_init)

    def tile_rows(t):
        return pl.ds(pl.multiple_of(t * tq, tq), tq)

    def scores(t, slot):
        q = q_ref[tile_rows(t), :]
        for i, sel in enumerate(map_lanes):
            qm = jnp.where(sel, q, jnp.zeros_like(q))
            s_ref[slot, i, 0:n_ctx, :] = _dot_nt(kc_ref[...], qm)
            if has_lat:
                s_ref[slot, i, n_ctx:, :] = _dot_nt(kl_ref[...], qm)

    def softmax_pv(t, slot):
        outs = []
        for i in range(2):
            s = s_ref[slot, i]
            e = jnp.exp2(s - jnp.max(s, axis=0, keepdims=True))
            l = jnp.sum(e, axis=0, keepdims=True)
            e = e.astype(BF16)
            o = _dot(vtc_ref[...], e[0:n_ctx])
            if has_lat:
                o = o + _dot(vtl_ref[...], e[n_ctx:])
            outs.append(o / l)
        ot = outs[0] - lam * outs[1]
        ms = jnp.mean(ot * ot, axis=0, keepdims=True)
        o_ref[tile_rows(t), :] = ((ot * lax.rsqrt(ms + EPS)).T * gain).astype(BF16)

    scores(0, 0)
    if nq > 1:
        def tile_pair(i, carry):
            t = 2 * i
            scores(t + 1, 1)
            softmax_pv(t, 0)
            scores(t + 2, 0)
            softmax_pv(t + 1, 1)
            return carry

        lax.fori_loop(0, nq // 2 - 1, tile_pair, 0)
        scores(nq - 1, 1)
        softmax_pv(nq - 2, 0)
        softmax_pv(nq - 1, 1)
    else:
        softmax_pv(0, 0)


def _diff_attention(q_lat, k_lat, vt_lat, q_ctx, k_ctx, vt_ctx, lqk, subln, lambda_init, batch, seq, ctx_len):
    w = q_lat.shape[1]
    assert (seq // TQ) % 2 == 0
    gain = subln.reshape(1, w)
    small = [pl.BlockSpec(lqk.shape, lambda b, h: (0, 0)),
             pl.BlockSpec((1, LANES), lambda b, h: (0, h))]
    ctx_k = pl.BlockSpec((ctx_len, LANES), lambda b, h: (b, h))
    ctx_vt = pl.BlockSpec((LANES, ctx_len), lambda b, h: (h, b))
    lat_k = pl.BlockSpec((seq, LANES), lambda b, h: (b, h))
    lat_vt = pl.BlockSpec((LANES, seq), lambda b, h: (h, b))
    o_lat = pl.pallas_call(
        functools.partial(_attn_kernel, lambda_init=lambda_init, has_lat=True),
        grid=(batch, DA_HEADS),
        in_specs=small + [lat_k, ctx_k, ctx_vt, lat_k, lat_vt],
        out_specs=lat_k,
        out_shape=jax.ShapeDtypeStruct(q_lat.shape, BF16),
        scratch_shapes=[pltpu.VMEM((2, 2, ctx_len + seq, TQ), F32)],
        compiler_params=_params(2),
        name="diff_attn_latent",
    )(lqk, gain, q_lat, k_ctx, vt_ctx, k_lat, vt_lat)
    o_ctx = pl.pallas_call(
        functools.partial(_attn_kernel, lambda_init=lambda_init, has_lat=False),
        grid=(batch, DA_HEADS),
        in_specs=small + [ctx_k, ctx_k, ctx_vt],
        out_specs=ctx_k,
        out_shape=jax.ShapeDtypeStruct(q_ctx.shape, BF16),
        scratch_shapes=[pltpu.VMEM((1, 2, ctx_len, ctx_len), F32)],
        compiler_params=_params(2),
        name="diff_attn_context",
    )(lqk, gain, q_ctx, k_ctx, vt_ctx)
    return o_lat, o_ctx


def _pool_kernel(u_ref, pw_ref, ps_ref, out_ref):
    n = u_ref.shape[0]
    t = lax.broadcasted_iota(jnp.int32, (n, 1), 0)
    for g, w in enumerate(POOL_WINDOWS):
        sl = slice(g * LANES, (g + 1) * LANES)
        ug = u_ref[:, sl]
        lo, hi = -(w // 2), w - w // 2
        acc = ug
        for off in range(lo, hi):
            if off == 0:
                continue
            valid = (t + off >= 0) & (t + off < n)
            acc = acc + jnp.where(valid, pltpu.roll(ug, (-off) % n, axis=0), 0.0)
        cnt = (jnp.minimum(t + hi, n) - jnp.maximum(t + lo, 0)).astype(F32)
        dlt = acc / cnt - ug
        y = _dot(dlt.astype(BF16), pw_ref[g]) * ps_ref[:, sl]
        out_ref[:, sl] = y.astype(BF16)


def _pool(u, pool_w, pool_scale, seq_len, name):
    rows, w = u.shape
    return pl.pallas_call(
        _pool_kernel,
        grid=(rows // seq_len,),
        in_specs=[pl.BlockSpec((seq_len, w), lambda b: (b, 0)),
                  pl.BlockSpec(pool_w.shape, lambda b: (0, 0, 0)),
                  pl.BlockSpec((1, w), lambda b: (0, 0))],
        out_specs=pl.BlockSpec((seq_len, w), lambda b: (b, 0)),
        out_shape=jax.ShapeDtypeStruct((rows, w), BF16),
        compiler_params=_params(1),
        name=name,
    )(u, pool_w, pool_scale.reshape(1, w))


def _ff_chunks(d_ff):
    tiles = d_ff // MXU_DIM
    assert tiles * MXU_DIM == d_ff
    n_chunks = -(-tiles // (FF_CHUNK // MXU_DIM))
    sizes = [(tiles // n_chunks + (j < tiles % n_chunks)) * MXU_DIM for j in range(n_chunks)]
    return [(sum(sizes[:j]), sizes[j]) for j in range(n_chunks)]


def _halo_specs(tm, width, rows, halo):
    tb = tm // halo
    last = rows // halo - 1
    return [pl.BlockSpec((tm, width), lambda i: (i, 0)),
            pl.BlockSpec((halo, width), lambda i: (jnp.maximum(i * tb - 1, 0), 0)),
            pl.BlockSpec((halo, width), lambda i: (jnp.minimum((i + 1) * tb, last), 0))]


def _with_halo(ref, prev_ref, next_ref):
    n = prev_ref.shape[0]
    prev = prev_ref[...].astype(F32)[n - SUBLANES:n]
    nxt = next_ref[...].astype(F32)[0:SUBLANES]
    return jnp.concatenate([prev, ref[...].astype(F32), nxt], axis=0)


def _layer_tail_kernel(*refs, mixer, tiles_per_seq, final):
    if mixer == "even":
        (x_ref, xp_ref, xn_ref, o_ref, op_ref, on_ref, p_ref, pp_ref, pn_ref, mod_ref, wo_ref,
         wu_ref, cw_ref, cb_ref, wd_ref, fin_ref, out_ref) = refs
    else:
        (x_ref, xp_ref, xn_ref, o_ref, op_ref, on_ref, g_ref, gp_ref, gn_ref, gain_ref, mod_ref, wo_ref,
         wu_ref, cw_ref, cb_ref, wd_ref, fin_ref, out_ref, hn_ref) = refs
    i = pl.program_id(0)
    tm = x_ref.shape[0]
    d_ff = wd_ref.shape[0]
    ext = tm + 2 * SUBLANES
    at_start = i % tiles_per_seq == 0
    at_end = (i + 1) % tiles_per_seq == 0
    m = mod_ref[0]

    if mixer == "even":
        k = o_ref.shape[1]
        pad = op_ref.shape[0] - SUBLANES
        o_ext = jnp.concatenate([op_ref[...], o_ref[...], on_ref[...]], axis=0)
        p_ext = jnp.concatenate([pp_ref[...], p_ref[...], pn_ref[...]], axis=0)
        y = (_dot(o_ext, wo_ref[0:k, :]) + _dot(p_ext, wo_ref[k:, :]))[pad:pad + ext]
    else:
        o_ext = _with_halo(o_ref, op_ref, on_ref)
        g_ext = _with_halo(g_ref, gp_ref, gn_ref)
        for h in range(o_ext.shape[1] // HG_HEAD_DIM):
            sl = slice(h * HG_HEAD_DIM, (h + 1) * HG_HEAD_DIM)
            o, g = o_ext[:, sl], g_ext[:, sl]
            ms = jnp.mean(o * o, axis=-1, keepdims=True)
            hn_ref[:, sl] = (o * lax.rsqrt(ms + EPS) * gain_ref[:, sl] * (g * _sigmoid(g))).astype(BF16)
        y = _dot(hn_ref[...], wo_ref[...])
    x1 = _with_halo(x_ref, xp_ref, xn_ref) + m[2:3] * y

    h = _modnorm(x1, m[3:4], m[4:5])
    h = jnp.concatenate([jnp.where(at_start, 0.0, h[0:SUBLANES]), h[SUBLANES:SUBLANES + tm],
                         jnp.where(at_end, 0.0, h[SUBLANES + tm:])], axis=0).astype(BF16)

    def conv(c0, n):
        u = _dot(h, wu_ref[:, c0:c0 + n])
        cw = cw_ref[:, c0:c0 + n]
        y = (pltpu.roll(u, 1, axis=0) * cw[0:1] + u * cw[1:2] + pltpu.roll(u, ext - 1, axis=0) * cw[2:3]
             + cb_ref[:, c0:c0 + n])
        return y[SUBLANES:SUBLANES + tm]

    acc = jnp.zeros((tm, out_ref.shape[1]), F32)
    for c0, n in _ff_chunks(d_ff):
        a = conv(c0, n)
        g = conv(d_ff + c0, n)
        act = (a * (g * _sigmoid(g))).astype(BF16)
        acc = acc + _dot(act, wd_ref[c0:c0 + n, :])
    y = x1[SUBLANES:SUBLANES + tm] + m[5:6] * acc
    if final:
        ms = jnp.mean(y * y, axis=-1, keepdims=True)
        y = y * lax.rsqrt(ms + EPS) * fin_ref[...]
    out_ref[...] = y


def _layer_tail(xs, mods, mixer, mix_in, w_out, ffn_w, tm, seq_len, mod_row, final, name):
    rows, d = xs.shape
    w_up, conv_w, conv_b, w_down, final_norm = ffn_w
    bf16_rows = 2 * SUBLANES
    in_specs = _halo_specs(tm, d, rows, SUBLANES)
    args = [xs] * 3
    scratch = []
    if mixer == "even":
        o, p = mix_in
        in_specs += _halo_specs(tm, o.shape[1], rows, bf16_rows) + _halo_specs(tm, p.shape[1], rows, bf16_rows)
        args += [o] * 3 + [p] * 3
    else:
        o, g, gain = mix_in
        w = o.shape[1]
        in_specs += _halo_specs(tm, w, rows, SUBLANES) + _halo_specs(tm, w, rows, bf16_rows) + [_resident((1, w))]
        args += [o] * 3 + [g] * 3 + [gain.reshape(1, w)]
        scratch = [pltpu.VMEM((tm + 2 * SUBLANES, w), BF16)]
    in_specs += [_mod_spec(d, tm, seq_len, mod_row), _resident(w_out.shape),
                 _resident(w_up.shape), _resident(conv_w.shape), _resident((1, conv_b.shape[0])),
                 _resident(w_down.shape), _resident((1, d))]
    args += [mods, w_out, w_up, conv_w, conv_b.reshape(1, -1), w_down, final_norm.reshape(1, d)]
    return pl.pallas_call(
        functools.partial(_layer_tail_kernel, mixer=mixer, tiles_per_seq=seq_len // tm, final=final),
        grid=(rows // tm,),
        in_specs=in_specs,
        out_specs=pl.BlockSpec((tm, d), lambda i: (i, 0)),
        out_shape=jax.ShapeDtypeStruct((rows, d), F32),
        scratch_shapes=scratch,
        compiler_params=_params(1),
        name=name,
    )(*args)


def _inproj_odd_kernel(x_ref, mod_ref, w_ref, *out_refs):
    m = mod_ref[0]
    h = _modnorm(x_ref[...], m[0:1], m[1:2]).astype(BF16)
    w = out_refs[0].shape[1]
    step = 512
    for c0 in range(0, w, step):
        sl = slice(c0, c0 + step)
        for n, ref in enumerate(out_refs):
            p = _dot(h, w_ref[:, n * w + c0:n * w + c0 + step])
            if n == 0:
                p = p * _sigmoid(p)
            ref[:, sl] = p.astype(ref.dtype)


def _inproj_odd(xs, mods, w_in, tm, seq_len, mod_row, want_gate, name):
    rows, d = xs.shape
    w = w_in.shape[1] // 5
    dtypes = [BF16, BF16, F32, F32] + ([BF16] if want_gate else [])
    return pl.pallas_call(
        _inproj_odd_kernel,
        grid=(rows // tm,),
        in_specs=[pl.BlockSpec((tm, d), lambda i: (i, 0)), _mod_spec(d, tm, seq_len, mod_row),
                  _resident(w_in.shape)],
        out_specs=[pl.BlockSpec((tm, w), lambda i: (i, 0))] * len(dtypes),
        out_shape=[jax.ShapeDtypeStruct((rows, w), dt) for dt in dtypes],
        compiler_params=_params(1),
        name=name,
    )(xs, mods, w_in)


def _cumsum_rows(tri, x):
    w = x.shape[1]
    hi = x.astype(BF16)
    r1 = x - hi.astype(F32)
    mid = r1.astype(BF16)
    lo = (r1 - mid.astype(F32)).astype(BF16)
    s = _dot(tri, jnp.concatenate([hi, mid, lo], axis=1))
    return s[:, 0:w] + s[:, w:2 * w] + s[:, 2 * w:3 * w]


def _hgrn2_kernel(lbl_ref, qc_ref, vc_ref, zfc_ref, zbc_ref, ql_ref, vl_ref, zfl_ref, zbl_ref,
                  o_ref, ob_ref, st_ref, *, layer):
    c = HG_CHUNK
    rb = HG_BLOCK
    n_sub = rb // c
    n_levels = n_sub.bit_length() - 1
    w = o_ref.shape[1]
    heads = w // HG_HEAD_DIM
    n_ctx = qc_ref.shape[0] // rb
    n_lat = ql_ref.shape[0] // rb
    r_i = lax.broadcasted_iota(jnp.int32, (rb, rb), 0)
    c_i = lax.broadcasted_iota(jnp.int32, (rb, rb), 1)
    same_group = [(r_i // (c << l)) == (c_i // (c << l)) for l in range(n_levels)]
    chunk_id = lax.broadcasted_iota(jnp.int32, (n_sub, 1, w), 0)
    hsl = [slice(h * HG_HEAD_DIM, (h + 1) * HG_HEAD_DIM) for h in range(heads)]

    def direction_constants(d):
        fwd = d == 0
        lg = lbl_ref[d]
        e = jnp.exp(lg - jnp.max(lg, axis=0, keepdims=True))
        p = e / jnp.sum(e, axis=0, keepdims=True)
        lb = jnp.sum(p[1:layer + 1], axis=0, keepdims=True)
        ordered = (c_i <= r_i) if fwd else (c_i >= r_i)
        tri = jnp.where(ordered, 1.0, 0.0).astype(BF16)
        return lb, same_group[0] & ordered, tri

    consts = [direction_constants(0), direction_constants(1)]

    def block(d, q, v, z, want_out, out):
        fwd = d == 0
        lb, causal, tri = consts[d]
        split = lambda a, n: a.reshape(rb // n, n, w)
        merge = lambda a: a.reshape(rb, w).astype(BF16)
        f = lb + (1.0 - lb) * _sigmoid(z)
        k = 1.0 - f
        logf = jnp.log(f)
        yield
        bg = _cumsum_rows(tri, logf)
        yield
        be = bg[rb - 1:rb] if fwd else bg[0:1]
        decay = jnp.exp(be)
        if want_out:
            b0 = split(bg, c)
            r0 = b0[:, c // 2 - 1:c // 2, :] if fwd else b0[:, c // 2:c // 2 + 1, :]
            qt = split(q.astype(F32), c) * jnp.exp(b0 - r0)
            kt = split(k, c) * jnp.exp(r0 - b0)
            q_lv, k_lv = [merge(qt)], [merge(kt)]
            for l in range(1, n_levels + 1):
                g = c << l
                bl = split(bg, g)
                m = bl[:, g // 2 - 1:g // 2, :] if fwd else bl[:, g // 2:g // 2 + 1, :]
                m = jnp.broadcast_to(m[:, None], (rb // g, 1 << l, 1, w)).reshape(n_sub, 1, w)
                late_half = ((chunk_id >> (l - 1)) & 1) == 1
                q_side = late_half if fwd else ~late_half
                q_lv.append(merge(qt * jnp.where(q_side, jnp.exp(r0 - m), 0.0)))
                k_lv.append(merge(kt * jnp.where(q_side, 0.0, jnp.exp(m - r0))))
            qh = merge(qt * jnp.exp(r0))
            kh = merge(kt * jnp.exp(be - r0))
        else:
            kh = (k * jnp.exp(be - bg)).astype(BF16)
        yield
        ds = [_dot_tn(v[:, sl], kh[:, sl]) for sl in hsl]
        if want_out:
            a_lv = [[_dot_nt(ql[:, sl], kl[:, sl]) for ql, kl in zip(q_lv, k_lv)] for sl in hsl]
        yield
        st_old = []
        for h in range(heads):
            st = st_ref[d, h]
            st_old.append(st.astype(BF16))
            st_ref[d, h] = st * decay[:, hsl[h]] + ds[h]
        if want_out:
            a = []
            for lv in a_lv:
                x = lv[n_levels]
                for l in range(n_levels - 1, 0, -1):
                    x = jnp.where(same_group[l], lv[l], x)
                a.append(jnp.where(causal, lv[0], x).astype(BF16))
        yield
        if want_out:
            out[d] = jnp.concatenate([_dot(a[h], v[:, hsl[h]]) + _dot_nt(qh[:, hsl[h]], st_old[h])
                                      for h in range(heads)], axis=1)
        yield

    def run_interleaved(gens):
        for _ in zip(*gens):
            pass

    def block_rows(d, j, n):
        return pl.ds(pl.multiple_of((j if d == 0 else n - 1 - j) * rb, rb), rb)

    def ctx_step(j, carry):
        gens = []
        for d, z_ref in ((0, zfc_ref), (1, zbc_ref)):
            rows = block_rows(d, j, n_ctx)
            gens.append(block(d, qc_ref[rows, :], vc_ref[rows, :], z_ref[rows, :], False, None))
        run_interleaved(gens)
        return carry

    def lat_step(j, carry):
        out = {}
        rows = [block_rows(d, j, n_lat) for d in (0, 1)]
        gens = [block(d, ql_ref[rows[d], :], vl_ref[rows[d], :], z_ref[rows[d], :], True, out)
                for d, z_ref in ((0, zfl_ref), (1, zbl_ref))]
        run_interleaved(gens)
        o_ref[rows[0], :] = out[0]
        ob_ref[rows[1], :] = out[1]
        return carry

    st_ref[...] = jnp.zeros_like(st_ref)
    lax.fori_loop(0, n_ctx, ctx_step, 0)
    lax.fori_loop(0, n_lat, lat_step, 0)
    o_ref[...] += ob_ref[...]


def _hgrn2(lat_in, ctx_in, lb_logits, layer, batch, seq, ctx_len):
    w = lat_in[0].shape[1]
    ctx = pl.BlockSpec((ctx_len, HG_LANES), lambda b, g: (b, g))
    lat = pl.BlockSpec((seq, HG_LANES), lambda b, g: (b, g))
    heads = HG_LANES // HG_HEAD_DIM
    return pl.pallas_call(
        functools.partial(_hgrn2_kernel, layer=layer),
        grid=(batch, w // HG_LANES),
        in_specs=[pl.BlockSpec((2, lb_logits.shape[1], HG_LANES), lambda b, g: (0, 0, g))] + [ctx] * 4 + [lat] * 4,
        out_specs=lat,
        out_shape=jax.ShapeDtypeStruct((batch * seq, w), F32),
        scratch_shapes=[pltpu.VMEM((seq, HG_LANES), F32),
                        pltpu.VMEM((2, heads, HG_HEAD_DIM, HG_HEAD_DIM), F32)],
        compiler_params=_params(2),
        name="hgrn2_scan",
    )(lb_logits, *ctx_in, *lat_in)


def kernel(x, c, ctx, c_ctx, w_mod, b_mod, ev_w_in, ev_w_out, da_lq1, da_lk1, da_lq2, da_lk2, da_subln, pool_w,
           pool_scale, hg_w_in, hg_w_out, hg_lb_logits, hg_norm, ffn_w_up, ffn_conv_w, ffn_conv_b, ffn_w_down,
           final_norm):
    batch, seq, d = x.shape
    ctx_len = ctx.shape[1]
    depth = w_mod.shape[0]
    assert seq % TM_LAT == 0 and seq % TQ == 0 and seq % HG_BLOCK == 0
    assert ctx_len % TM_CTX == 0 and ctx_len % HG_BLOCK == 0

    xl = x.reshape(batch * seq, d)
    xc = ctx.reshape(batch * ctx_len, d)
    cond_rows = 2 * SUBLANES
    cond = jnp.zeros((cond_rows, d), F32).at[:batch].set(c).at[batch].set(c_ctx)
    mods_all = _modulation(cond, w_mod, b_mod).reshape(depth, cond_rows, 6, d)
    lat = dict(tm=TM_LAT, seq_len=seq, mod_row=None)
    cxt = dict(tm=TM_CTX, seq_len=ctx_len, mod_row=batch)

    for i in range(depth):
        last = i == depth - 1
        mods = mods_all[i]
        j = i // 2
        ffn_w = (ffn_w_up[i].astype(BF16), ffn_conv_w[i], ffn_conv_b[i], ffn_w_down[i].astype(BF16), final_norm)
        if i % 2 == 0:
            lambda_init = 0.8 - 0.6 * math.exp(-0.3 * i)
            wq = ev_w_in[j].shape[1] // 4
            w_qku = jnp.concatenate([ev_w_in[j][:, :2 * wq], ev_w_in[j][:, 3 * wq:]], axis=1).astype(BF16)
            w_vt = ev_w_in[j][:, 2 * wq:3 * wq].T.astype(BF16)
            w_out = ev_w_out[j].astype(BF16)
            pw = pool_w[j].astype(BF16)
            lqk = jnp.stack([da_lq1[j], da_lk1[j], da_lq2[j], da_lk2[j]]).astype(F32)
            ql, kl, vtl, ul = _inproj_even(xl, mods, w_qku, w_vt, _rope_tables(seq), name="even_inproj_latent", **lat)
            qc, kc, vtc, uc = _inproj_even(xc, mods, w_qku, w_vt, None, name="even_inproj_context", **cxt)
            ol, oc = _diff_attention(ql, kl, vtl, qc, kc, vtc, lqk, da_subln[j], lambda_init, batch, seq, ctx_len)
            mix_lat = (ol, _pool(ul, pw, pool_scale[j], seq, "pool_latent"))
            mix_ctx = (oc, _pool(uc, pw, pool_scale[j], ctx_len, "pool_context"))
            mixer = "even"
        else:
            if not last:
                raise NotImplementedError("context readout of an HGRN2 layer that is not the last layer")
            w_in = hg_w_in[j].astype(BF16)
            *lat_in, gl = _inproj_odd(xl, mods, w_in, want_gate=True, name="odd_inproj_latent", **lat)
            ctx_in = _inproj_odd(xc, mods, w_in, want_gate=False, name="odd_inproj_context", **cxt)
            o = _hgrn2(lat_in, ctx_in, hg_lb_logits.astype(F32), i, batch, seq, ctx_len)
            mix_lat, mix_ctx = (o, gl, hg_norm[j]), None
            w_out = hg_w_out[j].astype(BF16)
            mixer = "odd"
        xl = _layer_tail(xl, mods, mixer, mix_lat, w_out, ffn_w, final=last, name="layer_tail_latent", **lat)
        if not last:
            xc = _layer_tail(xc, mods, mixer, mix_ctx, w_out, ffn_w, final=False, name="layer_tail_context", **cxt)
    return xl.reshape(batch, seq, d)
```

```python
import functools
import math

import jax
import jax.numpy as jnp
import numpy as np
from jax import lax
from jax.experimental import pallas as pl
from jax.experimental.pallas import tpu as pltpu

F32 = jnp.float32
BF16 = jnp.bfloat16

EPS = 1e-6
GRID_W = 64
DA_HEADS = 4
DA_QK_DIM = 64
ROPE_THETA = 10000.0
POOL_WINDOWS = (2, 4, 8, 16)
HG_HEAD_DIM = 128
LANES = 128
SUBLANES = 8
VMEM_LIMIT = 56 * 1024 * 1024

TM_LAT = 512
TM_CTX = 256
TQ = 256
ATTN_SLOTS = 3
HG_CHUNK = 32
HG_LANES = 256
HG_BLOCK = 256
MXU_DIM = 256
FF_CHUNK = 1536
LOG2E = math.log2(math.e)


def _resident(shape):
    nd = len(shape)
    return pl.BlockSpec(shape, lambda *_: (0,) * nd, pipeline_mode=pl.Buffered(1))


def _params(n_axes):
    return pltpu.CompilerParams(dimension_semantics=("parallel",) * n_axes, vmem_limit_bytes=VMEM_LIMIT)


def _mod_spec(d, tm, seq_len, fixed_row):
    if fixed_row is None:
        tiles_per_seq = seq_len // tm
        return pl.BlockSpec((1, 6, d), lambda i: (i // tiles_per_seq, 0, 0))
    return pl.BlockSpec((1, 6, d), lambda i: (fixed_row, 0, 0))


def _sigmoid(x):
    return 1.0 / (1.0 + jnp.exp(-x))


def _modnorm(x, shift, scale):
    ms = jnp.mean(x * x, axis=-1, keepdims=True)
    return (x * lax.rsqrt(ms + EPS)) * (1.0 + scale) + shift


def _dot(a, b):
    return jnp.dot(a, b, preferred_element_type=F32)


def _dot_nt(a, b):
    return lax.dot_general(a, b, (((1,), (1,)), ((), ())), preferred_element_type=F32)


def _dot_tn(a, b):
    return lax.dot_general(a, b, (((0,), (0,)), ((), ())), preferred_element_type=F32)


def _mod_kernel(cond_ref, w_ref, b_ref, out_ref):
    c = cond_ref[...]
    rows = c.shape[0]
    s = c * _sigmoid(c)
    hi = s.astype(BF16)
    lo = (s - hi.astype(F32)).astype(BF16)
    y = _dot(jnp.concatenate([hi, lo], axis=0), w_ref[0].astype(BF16))
    out_ref[0] = y[0:rows] + y[rows:] + b_ref[0]


def _modulation(cond, w_mod, b_mod):
    depth, d, n = w_mod.shape
    rows = cond.shape[0]
    tn = 2048
    return pl.pallas_call(
        _mod_kernel,
        grid=(depth, n // tn),
        in_specs=[pl.BlockSpec((rows, d), lambda l, j: (0, 0)),
                  pl.BlockSpec((1, d, tn), lambda l, j: (l, 0, j)),
                  pl.BlockSpec((1, 1, tn), lambda l, j: (l, 0, j))],
        out_specs=pl.BlockSpec((1, rows, tn), lambda l, j: (l, 0, j)),
        out_shape=jax.ShapeDtypeStruct((depth, rows, n), F32),
        compiler_params=_params(2),
        name="adaln_mod",
    )(cond, w_mod, b_mod.reshape(depth, 1, n))


def _inproj_even_kernel(*refs, has_rope):
    if has_rope:
        x_ref, mod_ref, w_ref, wvt_ref, cos_ref, sa_ref, sb_ref, q_ref, k_ref, vt_ref, u_ref = refs
    else:
        x_ref, mod_ref, w_ref, wvt_ref, q_ref, k_ref, vt_ref, u_ref = refs
    m = mod_ref[0]
    h = _modnorm(x_ref[...], m[0:1], m[1:2]).astype(BF16)
    w = q_ref.shape[1]
    pq = _dot(h, w_ref[:, 0:w])
    pk = _dot(h, w_ref[:, w:2 * w])
    if has_rope:
        reps = w // LANES
        cos = jnp.concatenate([cos_ref[...]] * reps, axis=1)
        sa = jnp.concatenate([sa_ref[...]] * reps, axis=1)
        sb = jnp.concatenate([sb_ref[...]] * reps, axis=1)
        half = DA_QK_DIM // 4

        def rope(p):
            return p * cos + pltpu.roll(p, w - half, axis=1) * sa + pltpu.roll(p, half, axis=1) * sb

        pq, pk = rope(pq), rope(pk)
    q_ref[...] = (pq * (DA_QK_DIM ** -0.5 * LOG2E)).astype(BF16)
    k_ref[...] = pk.astype(BF16)
    vt_ref[...] = _dot_nt(wvt_ref[...], h).astype(BF16)
    u_ref[...] = _dot(h, w_ref[:, 2 * w:3 * w])


def _inproj_even(xs, mods, w_qku, w_vt, rope_tabs, tm, seq_len, mod_row, name):
    rows, d = xs.shape
    w = w_vt.shape[0]
    has_rope = rope_tabs is not None
    in_specs = [pl.BlockSpec((tm, d), lambda i: (i, 0)), _mod_spec(d, tm, seq_len, mod_row),
                _resident(w_qku.shape), _resident(w_vt.shape)]
    args = [xs, mods, w_qku, w_vt]
    if has_rope:
        tiles_per_seq = seq_len // tm
        in_specs += [pl.BlockSpec((tm, LANES), lambda i: (i % tiles_per_seq, 0))] * 3
        args += list(rope_tabs)
    tok = lambda dt: jax.ShapeDtypeStruct((rows, w), dt)
    tok_spec = pl.BlockSpec((tm, w), lambda i: (i, 0))
    return pl.pallas_call(
        functools.partial(_inproj_even_kernel, has_rope=has_rope),
        grid=(rows // tm,),
        in_specs=in_specs,
        out_specs=[tok_spec, tok_spec, pl.BlockSpec((w, tm), lambda i: (0, i)), tok_spec],
        out_shape=[tok(BF16), tok(BF16), jax.ShapeDtypeStruct((w, rows), BF16), tok(F32)],
        compiler_params=_params(1),
        name=name,
    )(*args)


def _rope_tables(seq):
    rows_n = seq // GRID_W
    row = np.repeat(np.arange(rows_n, dtype=np.float64), GRID_W)
    col = np.tile(np.arange(GRID_W, dtype=np.float64), rows_n)
    n_freq = DA_QK_DIM // 4
    inv = ROPE_THETA ** (-np.arange(n_freq, dtype=np.float64) / n_freq)
    ang_r, ang_c = row[:, None] * inv, col[:, None] * inv
    zeros = np.zeros_like(ang_r)
    cos64 = np.concatenate([np.cos(ang_r)] * 2 + [np.cos(ang_c)] * 2, axis=1)
    sa64 = np.concatenate([-np.sin(ang_r), zeros, -np.sin(ang_c), zeros], axis=1)
    sb64 = np.concatenate([zeros, np.sin(ang_r), zeros, np.sin(ang_c)], axis=1)
    return tuple(jnp.asarray(np.concatenate([t, t], axis=1), dtype=F32) for t in (cos64, sa64, sb64))


def _attn_kernel(*refs, lambda_init, has_lat):
    if has_lat:
        lqk_ref, gain_ref, q_ref, kc_ref, vtc_ref, kl_ref, vtl_ref, o_ref, s_ref, vta_ref = refs
    else:
        lqk_ref, gain_ref, q_ref, kc_ref, vtc_ref, o_ref, s_ref, vta_ref = refs
    tq = s_ref.shape[3]
    nq = q_ref.shape[0] // tq
    n_ctx = kc_ref.shape[0]
    dv = vtc_ref.shape[0]
    vta_ref[0:dv, 0:n_ctx] = vtc_ref[...]
    if has_lat:
        vta_ref[0:dv, n_ctx:] = vtl_ref[...]
    ones_row = lax.broadcasted_iota(jnp.int32, (vta_ref.shape[0] - dv, vta_ref.shape[1]), 0) == 0
    vta_ref[dv:, :] = jnp.where(ones_row, 1.0, 0.0).astype(BF16)
    a = lqk_ref[...]
    lam = (jnp.exp(jnp.sum(a[0:1] * a[1:2], axis=-1, keepdims=True))
           - jnp.exp(jnp.sum(a[2:3] * a[3:4], axis=-1, keepdims=True)) + lambda_init)
    lane = lax.broadcasted_iota(jnp.int32, (1, LANES), 1)
    map_lanes = [lane < DA_QK_DIM, lane >= DA_QK_DIM]
    gain = gain_ref[...] * (1.0 - lambda_init)

    def tile_rows(t):
        return pl.ds(t * tq, tq)

    def scores(t, slot):
        q = q_ref[tile_rows(t), :]
        for i, sel in enumerate(map_lanes):
            qm = jnp.where(sel, q, jnp.zeros_like(q))
            s_ref[slot, i, 0:n_ctx, :] = _dot_nt(kc_ref[...], qm)
            if has_lat:
                s_ref[slot, i, n_ctx:, :] = _dot_nt(kl_ref[...], qm)

    def softmax_pv(t, slot):
        outs = []
        for i in range(2):
            s = s_ref[slot, i]
            e = jnp.exp2(s - jnp.max(s, axis=0, keepdims=True)).astype(BF16)
            oa = _dot(vta_ref[...], e)
            outs.append(oa[0:dv] / oa[dv:dv + 1])
        ot = outs[0] - lam * outs[1]
        ms = jnp.mean(ot * ot, axis=0, keepdims=True)
        o_ref[tile_rows(t), :] = ((ot * lax.rsqrt(ms + EPS)).T * gain).astype(BF16)

    n_slots = s_ref.shape[0]
    for t in range(min(n_slots - 1, nq)):
        scores(t, t % n_slots)
    for t in range(nq):
        if t + n_slots - 1 < nq:
            scores(t + n_slots - 1, (t + n_slots - 1) % n_slots)
        softmax_pv(t, t % n_slots)


def _diff_attention(q_lat, k_lat, vt_lat, q_ctx, k_ctx, vt_ctx, lqk, subln, lambda_init, batch, seq, ctx_len):
    w = q_lat.shape[1]
    assert (seq // TQ) % 2 == 0
    gain = subln.reshape(1, w)
    small = [pl.BlockSpec(lqk.shape, lambda b, h: (0, 0)),
             pl.BlockSpec((1, LANES), lambda b, h: (0, h))]
    ctx_k = pl.BlockSpec((ctx_len, LANES), lambda b, h: (b, h))
    ctx_vt = pl.BlockSpec((LANES, ctx_len), lambda b, h: (h, b))
    lat_k = pl.BlockSpec((seq, LANES), lambda b, h: (b, h))
    lat_vt = pl.BlockSpec((LANES, seq), lambda b, h: (h, b))
    o_lat = pl.pallas_call(
        functools.partial(_attn_kernel, lambda_init=lambda_init, has_lat=True),
        grid=(batch, DA_HEADS),
        in_specs=small + [lat_k, ctx_k, ctx_vt, lat_k, lat_vt],
        out_specs=lat_k,
        out_shape=jax.ShapeDtypeStruct(q_lat.shape, BF16),
        scratch_shapes=[pltpu.VMEM((ATTN_SLOTS, 2, ctx_len + seq, TQ), F32),
                        pltpu.VMEM((LANES + 2 * SUBLANES, ctx_len + seq), BF16)],
        compiler_params=_params(2),
        name="diff_attn_latent",
    )(lqk, gain, q_lat, k_ctx, vt_ctx, k_lat, vt_lat)
    o_ctx = pl.pallas_call(
        functools.partial(_attn_kernel, lambda_init=lambda_init, has_lat=False),
        grid=(batch, DA_HEADS),
        in_specs=small + [ctx_k, ctx_k, ctx_vt],
        out_specs=ctx_k,
        out_shape=jax.ShapeDtypeStruct(q_ctx.shape, BF16),
        scratch_shapes=[pltpu.VMEM((1, 2, ctx_len, ctx_len), F32),
                        pltpu.VMEM((LANES + 2 * SUBLANES, ctx_len), BF16)],
        compiler_params=_params(2),
        name="diff_attn_context",
    )(lqk, gain, q_ctx, k_ctx, vt_ctx)
    return o_lat, o_ctx


def _pool_kernel(u_ref, pw_ref, ps_ref, out_ref):
    n = u_ref.shape[0]
    t = lax.broadcasted_iota(jnp.int32, (n, 1), 0)
    for g, w in enumerate(POOL_WINDOWS):
        sl = slice(g * LANES, (g + 1) * LANES)
        ug = u_ref[:, sl]
        lo, hi = -(w // 2), w - w // 2
        acc = ug
        for off in range(lo, hi):
            if off == 0:
                continue
            valid = (t + off >= 0) & (t + off < n)
            acc = acc + jnp.where(valid, pltpu.roll(ug, (-off) % n, axis=0), 0.0)
        cnt = (jnp.minimum(t + hi, n) - jnp.maximum(t + lo, 0)).astype(F32)
        dlt = acc / cnt - ug
        y = _dot(dlt.astype(BF16), pw_ref[g]) * ps_ref[:, sl]
        out_ref[:, sl] = y.astype(BF16)


def _pool(u, pool_w, pool_scale, seq_len, name):
    rows, w = u.shape
    return pl.pallas_call(
        _pool_kernel,
        grid=(rows // seq_len,),
        in_specs=[pl.BlockSpec((seq_len, w), lambda b: (b, 0)),
                  pl.BlockSpec(pool_w.shape, lambda b: (0, 0, 0)),
                  pl.BlockSpec((1, w), lambda b: (0, 0))],
        out_specs=pl.BlockSpec((seq_len, w), lambda b: (b, 0)),
        out_shape=jax.ShapeDtypeStruct((rows, w), BF16),
        compiler_params=_params(1),
        name=name,
    )(u, pool_w, pool_scale.reshape(1, w))


def _ff_chunks(d_ff):
    tiles = d_ff // MXU_DIM
    assert tiles * MXU_DIM == d_ff
    n_chunks = -(-tiles // (FF_CHUNK // MXU_DIM))
    sizes = [(tiles // n_chunks + (j < tiles % n_chunks)) * MXU_DIM for j in range(n_chunks)]
    return [(sum(sizes[:j]), sizes[j]) for j in range(n_chunks)]


def _halo_specs(tm, width, rows, halo):
    tb = tm // halo
    last = rows // halo - 1
    return [pl.BlockSpec((tm, width), lambda i: (i, 0)),
            pl.BlockSpec((halo, width), lambda i: (jnp.maximum(i * tb - 1, 0), 0)),
            pl.BlockSpec((halo, width), lambda i: (jnp.minimum((i + 1) * tb, last), 0))]


def _with_halo(ref, prev_ref, next_ref):
    n = prev_ref.shape[0]
    prev = prev_ref[...].astype(F32)[n - SUBLANES:n]
    nxt = next_ref[...].astype(F32)[0:SUBLANES]
    return jnp.concatenate([prev, ref[...].astype(F32), nxt], axis=0)


def _layer_tail_kernel(*refs, mixer, tiles_per_seq, final):
    if mixer == "even":
        (x_ref, xp_ref, xn_ref, o_ref, op_ref, on_ref, p_ref, pp_ref, pn_ref, mod_ref, wo_ref,
         wu_ref, cw_ref, cb_ref, wd_ref, fin_ref, out_ref) = refs
    else:
        (x_ref, xp_ref, xn_ref, o_ref, op_ref, on_ref, g_ref, gp_ref, gn_ref, gain_ref, mod_ref, wo_ref,
         wu_ref, cw_ref, cb_ref, wd_ref, fin_ref, out_ref, hn_ref) = refs
    i = pl.program_id(0)
    tm = x_ref.shape[0]
    d_ff = wd_ref.shape[0]
    ext = tm + 2 * SUBLANES
    at_start = i % tiles_per_seq == 0
    at_end = (i + 1) % tiles_per_seq == 0
    m = mod_ref[0]

    if mixer == "even":
        k = o_ref.shape[1]
        pad = op_ref.shape[0] - SUBLANES
        o_ext = jnp.concatenate([op_ref[...], o_ref[...], on_ref[...]], axis=0)
        p_ext = jnp.concatenate([pp_ref[...], p_ref[...], pn_ref[...]], axis=0)
        y = (_dot(o_ext, wo_ref[0:k, :]) + _dot(p_ext, wo_ref[k:, :]))[pad:pad + ext]
    else:
        o_ext = _with_halo(o_ref, op_ref, on_ref)
        g_ext = _with_halo(g_ref, gp_ref, gn_ref)
        for h in range(o_ext.shape[1] // HG_HEAD_DIM):
            sl = slice(h * HG_HEAD_DIM, (h + 1) * HG_HEAD_DIM)
            o, g = o_ext[:, sl], g_ext[:, sl]
            ms = jnp.mean(o * o, axis=-1, keepdims=True)
            hn_ref[:, sl] = (o * lax.rsqrt(ms + EPS) * gain_ref[:, sl] * (g * _sigmoid(g))).astype(BF16)
        y = _dot(hn_ref[...], wo_ref[...])
    x1 = _with_halo(x_ref, xp_ref, xn_ref) + m[2:3] * y

    h = _modnorm(x1, m[3:4], m[4:5])
    h = jnp.concatenate([jnp.where(at_start, 0.0, h[0:SUBLANES]), h[SUBLANES:SUBLANES + tm],
                         jnp.where(at_end, 0.0, h[SUBLANES + tm:])], axis=0).astype(BF16)

    def conv(c0, n):
        u = _dot(h, wu_ref[:, c0:c0 + n])
        cw = cw_ref[:, c0:c0 + n]
        y = (pltpu.roll(u, 1, axis=0) * cw[0:1] + u * cw[1:2] + pltpu.roll(u, ext - 1, axis=0) * cw[2:3]
             + cb_ref[:, c0:c0 + n])
        return y[SUBLANES:SUBLANES + tm]

    acc = jnp.zeros((tm, out_ref.shape[1]), F32)
    for c0, n in _ff_chunks(d_ff):
        a = conv(c0, n)
        g = conv(d_ff + c0, n)
        act = (a * (g * _sigmoid(g))).astype(BF16)
        acc = acc + _dot(act, wd_ref[c0:c0 + n, :])
    y = x1[SUBLANES:SUBLANES + tm] + m[5:6] * acc
    if final:
        ms = jnp.mean(y * y, axis=-1, keepdims=True)
        y = y * lax.rsqrt(ms + EPS) * fin_ref[...]
    out_ref[...] = y


def _layer_tail(xs, mods, mixer, mix_in, w_out, ffn_w, tm, seq_len, mod_row, final, name):
    rows, d = xs.shape
    w_up, conv_w, conv_b, w_down, final_norm = ffn_w
    bf16_rows = 2 * SUBLANES
    in_specs = _halo_specs(tm, d, rows, SUBLANES)
    args = [xs] * 3
    scratch = []
    if mixer == "even":
        o, p = mix_in
        in_specs += _halo_specs(tm, o.shape[1], rows, bf16_rows) + _halo_specs(tm, p.shape[1], rows, bf16_rows)
        args += [o] * 3 + [p] * 3
    else:
        o, g, gain = mix_in
        w = o.shape[1]
        in_specs += _halo_specs(tm, w, rows, SUBLANES) + _halo_specs(tm, w, rows, bf16_rows) + [_resident((1, w))]
        args += [o] * 3 + [g] * 3 + [gain.reshape(1, w)]
        scratch = [pltpu.VMEM((tm + 2 * SUBLANES, w), BF16)]
    in_specs += [_mod_spec(d, tm, seq_len, mod_row), _resident(w_out.shape),
                 _resident(w_up.shape), _resident(conv_w.shape), _resident((1, conv_b.shape[0])),
                 _resident(w_down.shape), _resident((1, d))]
    args += [mods, w_out, w_up, conv_w, conv_b.reshape(1, -1), w_down, final_norm.reshape(1, d)]
    return pl.pallas_call(
        functools.partial(_layer_tail_kernel, mixer=mixer, tiles_per_seq=seq_len // tm, final=final),
        grid=(rows // tm,),
        in_specs=in_specs,
        out_specs=pl.BlockSpec((tm, d), lambda i: (i, 0)),
        out_shape=jax.ShapeDtypeStruct((rows, d), F32),
        scratch_shapes=scratch,
        compiler_params=_params(1),
        name=name,
    )(*args)


def _inproj_odd_kernel(x_ref, mod_ref, w_ref, *out_refs):
    m = mod_ref[0]
    h = _modnorm(x_ref[...], m[0:1], m[1:2]).astype(BF16)
    w = out_refs[0].shape[1]
    step = 512
    for c0 in range(0, w, step):
        sl = slice(c0, c0 + step)
        for n, ref in enumerate(out_refs):
            p = _dot(h, w_ref[:, n * w + c0:n * w + c0 + step])
            if n == 0:
                p = p * _sigmoid(p)
            ref[:, sl] = p.astype(ref.dtype)


def _inproj_odd(xs, mods, w_in, tm, seq_len, mod_row, want_gate, name):
    rows, d = xs.shape
    w = w_in.shape[1] // 5
    dtypes = [BF16, BF16, F32, F32] + ([BF16] if want_gate else [])
    return pl.pallas_call(
        _inproj_odd_kernel,
        grid=(rows // tm,),
        in_specs=[pl.BlockSpec((tm, d), lambda i: (i, 0)), _mod_spec(d, tm, seq_len, mod_row),
                  _resident(w_in.shape)],
        out_specs=[pl.BlockSpec((tm, w), lambda i: (i, 0))] * len(dtypes),
        out_shape=[jax.ShapeDtypeStruct((rows, w), dt) for dt in dtypes],
        compiler_params=_params(1),
        name=name,
    )(xs, mods, w_in)


def _cumsum_rows(tri, x):
    w = x.shape[1]
    hi = x.astype(BF16)
    r1 = x - hi.astype(F32)
    mid = r1.astype(BF16)
    lo = (r1 - mid.astype(F32)).astype(BF16)
    s = _dot(tri, jnp.concatenate([hi, mid, lo], axis=1))
    return s[:, 0:w] + s[:, w:2 * w] + s[:, 2 * w:3 * w]


def _hgrn2_kernel(lbl_ref, qc_ref, vc_ref, zfc_ref, zbc_ref, ql_ref, vl_ref, zfl_ref, zbl_ref,
                  o_ref, ob_ref, st_ref, *, layer):
    c = HG_CHUNK
    rb = HG_BLOCK
    n_sub = rb // c
    n_levels = n_sub.bit_length() - 1
    w = o_ref.shape[1]
    heads = w // HG_HEAD_DIM
    n_ctx = qc_ref.shape[0] // rb
    n_lat = ql_ref.shape[0] // rb
    r_i = lax.broadcasted_iota(jnp.int32, (rb, rb), 0)
    c_i = lax.broadcasted_iota(jnp.int32, (rb, rb), 1)
    same_group = [(r_i // (c << l)) == (c_i // (c << l)) for l in range(n_levels)]
    chunk_id = lax.broadcasted_iota(jnp.int32, (n_sub, 1, w), 0)
    hsl = [slice(h * HG_HEAD_DIM, (h + 1) * HG_HEAD_DIM) for h in range(heads)]

    def direction_constants(d):
        fwd = d == 0
        lg = lbl_ref[d]
        e = jnp.exp(lg - jnp.max(lg, axis=0, keepdims=True))
        p = e / jnp.sum(e, axis=0, keepdims=True)
        lb = jnp.sum(p[1:layer + 1], axis=0, keepdims=True)
        ordered = (c_i <= r_i) if fwd else (c_i >= r_i)
        tri = jnp.where(ordered, 1.0, 0.0).astype(BF16)
        return lb, same_group[0] & ordered, tri

    consts = [direction_constants(0), direction_constants(1)]

    def block(d, q, v, z, want_out, out):
        fwd = d == 0
        lb, causal, tri = consts[d]
        split = lambda a, n: a.reshape(rb // n, n, w)
        merge = lambda a: a.reshape(rb, w).astype(BF16)
        f = lb + (1.0 - lb) * _sigmoid(z)
        k = 1.0 - f
        logf = jnp.log(f)
        yield
        bg = _cumsum_rows(tri, logf)
        yield
        be = bg[rb - 1:rb] if fwd else bg[0:1]
        decay = jnp.exp(be)
        if want_out:
            b0 = split(bg, c)
            r0 = b0[:, c // 2 - 1:c // 2, :] if fwd else b0[:, c // 2:c // 2 + 1, :]
            qt = split(q.astype(F32), c) * jnp.exp(b0 - r0)
            kt = split(k, c) * jnp.exp(r0 - b0)
            q_lv, k_lv = [merge(qt)], [merge(kt)]
            for l in range(1, n_levels + 1):
                g = c << l
                bl = split(bg, g)
                m = bl[:, g // 2 - 1:g // 2, :] if fwd else bl[:, g // 2:g // 2 + 1, :]
                m = jnp.broadcast_to(m[:, None], (rb // g, 1 << l, 1, w)).reshape(n_sub, 1, w)
                late_half = ((chunk_id >> (l - 1)) & 1) == 1
                q_side = late_half if fwd else ~late_half
                q_lv.append(merge(qt * jnp.where(q_side, jnp.exp(r0 - m), 0.0)))
                k_lv.append(merge(kt * jnp.where(q_side, 0.0, jnp.exp(m - r0))))
            qh = merge(qt * jnp.exp(r0))
            kh = merge(kt * jnp.exp(be - r0))
        else:
            kh = (k * jnp.exp(be - bg)).astype(BF16)
        yield
        ds = [_dot_tn(v[:, sl], kh[:, sl]) for sl in hsl]
        if want_out:
            a_lv = [[_dot_nt(ql[:, sl], kl[:, sl]) for ql, kl in zip(q_lv, k_lv)] for sl in hsl]
        yield
        st_old = []
        for h in range(heads):
            st = st_ref[d, h]
            st_old.append(st.astype(BF16))
            st_ref[d, h] = st * decay[:, hsl[h]] + ds[h]
        if want_out:
            a = []
            for lv in a_lv:
                x = lv[n_levels]
                for l in range(n_levels - 1, 0, -1):
                    x = jnp.where(same_group[l], lv[l], x)
                a.append(jnp.where(causal, lv[0], x).astype(BF16))
        yield
        if want_out:
            out[d] = jnp.concatenate([_dot(a[h], v[:, hsl[h]]) + _dot_nt(qh[:, hsl[h]], st_old[h])
                                      for h in range(heads)], axis=1)
        yield

    def run_interleaved(gens):
        for _ in zip(*gens):
            pass

    def block_rows(d, j, n):
        return pl.ds(pl.multiple_of((j if d == 0 else n - 1 - j) * rb, rb), rb)

    def ctx_step(j, carry):
        gens = []
        for d, z_ref in ((0, zfc_ref), (1, zbc_ref)):
            rows = block_rows(d, j, n_ctx)
            gens.append(block(d, qc_ref[rows, :], vc_ref[rows, :], z_ref[rows, :], False, None))
        run_interleaved(gens)
        return carry

    def lat_step(j, carry):
        out = {}
        rows = [block_rows(d, j, n_lat) for d in (0, 1)]
        gens = [block(d, ql_ref[rows[d], :], vl_ref[rows[d], :], z_ref[rows[d], :], True, out)
                for d, z_ref in ((0, zfl_ref), (1, zbl_ref))]
        run_interleaved(gens)
        o_ref[rows[0], :] = out[0]
        ob_ref[rows[1], :] = out[1]
        return carry

    st_ref[...] = jnp.zeros_like(st_ref)
    lax.fori_loop(0, n_ctx, ctx_step, 0)
    lax.fori_loop(0, n_lat, lat_step, 0)
    o_ref[...] += ob_ref[...]


def _hgrn2(lat_in, ctx_in, lb_logits, layer, batch, seq, ctx_len):
    w = lat_in[0].shape[1]
    ctx = pl.BlockSpec((ctx_len, HG_LANES), lambda b, g: (b, g))
    lat = pl.BlockSpec((seq, HG_LANES), lambda b, g: (b, g))
    heads = HG_LANES // HG_HEAD_DIM
    return pl.pallas_call(
        functools.partial(_hgrn2_kernel, layer=layer),
        grid=(batch, w // HG_LANES),
        in_specs=[pl.BlockSpec((2, lb_logits.shape[1], HG_LANES), lambda b, g: (0, 0, g))] + [ctx] * 4 + [lat] * 4,
        out_specs=lat,
        out_shape=jax.ShapeDtypeStruct((batch * seq, w), F32),
        scratch_shapes=[pltpu.VMEM((seq, HG_LANES), F32),
                        pltpu.VMEM((2, heads, HG_HEAD_DIM, HG_HEAD_DIM), F32)],
        compiler_params=_params(2),
        name="hgrn2_scan",
    )(lb_logits, *ctx_in, *lat_in)


def kernel(x, c, ctx, c_ctx, w_mod, b_mod, ev_w_in, ev_w_out, da_lq1, da_lk1, da_lq2, da_lk2, da_subln, pool_w,
           pool_scale, hg_w_in, hg_w_out, hg_lb_logits, hg_norm, ffn_w_up, ffn_conv_w, ffn_conv_b, ffn_w_down,
           final_norm):
    batch, seq, d = x.shape
    ctx_len = ctx.shape[1]
    depth = w_mod.shape[0]
    assert seq % TM_LAT == 0 and seq % TQ == 0 and seq % HG_BLOCK == 0
    assert ctx_len % TM_CTX == 0 and ctx_len % HG_BLOCK == 0

    xl = x.reshape(batch * seq, d)
    xc = ctx.reshape(batch * ctx_len, d)
    cond_rows = 2 * SUBLANES
    cond = jnp.zeros((cond_rows, d), F32).at[:batch].set(c).at[batch].set(c_ctx)
    mods_all = _modulation(cond, w_mod, b_mod).reshape(depth, cond_rows, 6, d)
    lat = dict(tm=TM_LAT, seq_len=seq, mod_row=None)
    cxt = dict(tm=TM_CTX, seq_len=ctx_len, mod_row=batch)

    for i in range(depth):
        last = i == depth - 1
        mods = mods_all[i]
        j = i // 2
        ffn_w = (ffn_w_up[i].astype(BF16), ffn_conv_w[i], ffn_conv_b[i], ffn_w_down[i].astype(BF16), final_norm)
        if i % 2 == 0:
            lambda_init = 0.8 - 0.6 * math.exp(-0.3 * i)
            wq = ev_w_in[j].shape[1] // 4
            w_qku = jnp.concatenate([ev_w_in[j][:, :2 * wq], ev_w_in[j][:, 3 * wq:]], axis=1).astype(BF16)
            w_vt = ev_w_in[j][:, 2 * wq:3 * wq].T.astype(BF16)
            w_out = ev_w_out[j].astype(BF16)
            pw = pool_w[j].astype(BF16)
            lqk = jnp.stack([da_lq1[j], da_lk1[j], da_lq2[j], da_lk2[j]]).astype(F32)
            ql, kl, vtl, ul = _inproj_even(xl, mods, w_qku, w_vt, _rope_tables(seq), name="even_inproj_latent", **lat)
            qc, kc, vtc, uc = _inproj_even(xc, mods, w_qku, w_vt, None, name="even_inproj_context", **cxt)
            ol, oc = _diff_attention(ql, kl, vtl, qc, kc, vtc, lqk, da_subln[j], lambda_init, batch, seq, ctx_len)
            mix_lat = (ol, _pool(ul, pw, pool_scale[j], seq, "pool_latent"))
            mix_ctx = (oc, _pool(uc, pw, pool_scale[j], ctx_len, "pool_context"))
            mixer = "even"
        else:
            if not last:
                raise NotImplementedError("context readout of an HGRN2 layer that is not the last layer")
            w_in = hg_w_in[j].astype(BF16)
            *lat_in, gl = _inproj_odd(xl, mods, w_in, want_gate=True, name="odd_inproj_latent", **lat)
            ctx_in = _inproj_odd(xc, mods, w_in, want_gate=False, name="odd_inproj_context", **cxt)
            o = _hgrn2(lat_in, ctx_in, hg_lb_logits.astype(F32), i, batch, seq, ctx_len)
            mix_lat, mix_ctx = (o, gl, hg_norm[j]), None
            w_out = hg_w_out[j].astype(BF16)
            mixer = "odd"
        xl = _layer_tail(xl, mods, mixer, mix_lat, w_out, ffn_w, final=last, name="layer_tail_latent", **lat)
        if not last:
            xc = _layer_tail(xc, mods, mixer, mix_ctx, w_out, ffn_w, final=False, name="layer_tail_context", **cxt)
    return xl.reshape(batch, seq, d)
```

```python
import functools
import math

import jax
import jax.numpy as jnp
import numpy as np
from jax import lax
from jax.experimental import pallas as pl
from jax.experimental.pallas import tpu as pltpu

F32 = jnp.float32
BF16 = jnp.bfloat16

EPS = 1e-6
GRID_W = 64
DA_HEADS = 4
DA_QK_DIM = 64
ROPE_THETA = 10000.0
POOL_WINDOWS = (2, 4, 8, 16)
HG_HEAD_DIM = 128
LANES = 128
SUBLANES = 8
VMEM_LIMIT = 56 * 1024 * 1024

TM_LAT = 512
TM_CTX = 256
TQ = 256
ATTN_SLOTS = 3
HG_CHUNK = 32
HG_LANES = 256
HG_BLOCK = 256
MXU_DIM = 256
FF_CHUNK = 1536
LOG2E = math.log2(math.e)


def _resident(shape):
    nd = len(shape)
    return pl.BlockSpec(shape, lambda *_: (0,) * nd, pipeline_mode=pl.Buffered(1))


def _params(n_axes):
    return pltpu.CompilerParams(dimension_semantics=("parallel",) * n_axes, vmem_limit_bytes=VMEM_LIMIT)


def _mod_spec(d, tm, seq_len, fixed_row):
    if fixed_row is None:
        tiles_per_seq = seq_len // tm
        return pl.BlockSpec((1, 6, d), lambda i: (i // tiles_per_seq, 0, 0))
    return pl.BlockSpec((1, 6, d), lambda i: (fixed_row, 0, 0))


def _sigmoid(x):
    return 1.0 / (1.0 + jnp.exp(-x))


def _modnorm(x, shift, scale):
    ms = jnp.mean(x * x, axis=-1, keepdims=True)
    return (x * lax.rsqrt(ms + EPS)) * (1.0 + scale) + shift


def _dot(a, b):
    return jnp.dot(a, b, preferred_element_type=F32)


def _dot_nt(a, b):
    return lax.dot_general(a, b, (((1,), (1,)), ((), ())), preferred_element_type=F32)


def _dot_tn(a, b):
    return lax.dot_general(a, b, (((0,), (0,)), ((), ())), preferred_element_type=F32)


def _mod_kernel(cond_ref, w_ref, b_ref, out_ref):
    c = cond_ref[...]
    rows = c.shape[0]
    s = c * _sigmoid(c)
    hi = s.astype(BF16)
    lo = (s - hi.astype(F32)).astype(BF16)
    y = _dot(jnp.concatenate([hi, lo], axis=0), w_ref[0].astype(BF16))
    out_ref[0] = y[0:rows] + y[rows:] + b_ref[0]


def _modulation(cond, w_mod, b_mod):
    depth, d, n = w_mod.shape
    rows = cond.shape[0]
    tn = 2048
    return pl.pallas_call(
        _mod_kernel,
        grid=(depth, n // tn),
        in_specs=[pl.BlockSpec((rows, d), lambda l, j: (0, 0)),
                  pl.BlockSpec((1, d, tn), lambda l, j: (l, 0, j)),
                  pl.BlockSpec((1, 1, tn), lambda l, j: (l, 0, j))],
        out_specs=pl.BlockSpec((1, rows, tn), lambda l, j: (l, 0, j)),
        out_shape=jax.ShapeDtypeStruct((depth, rows, n), F32),
        compiler_params=_params(2),
        name="adaln_mod",
    )(cond, w_mod, b_mod.reshape(depth, 1, n))


def _inproj_even_kernel(*refs, has_rope):
    if has_rope:
        x_ref, mod_ref, w_ref, wvt_ref, cos_ref, sa_ref, sb_ref, q_ref, k_ref, vt_ref, u_ref = refs
    else:
        x_ref, mod_ref, w_ref, wvt_ref, q_ref, k_ref, vt_ref, u_ref = refs
    m = mod_ref[0]
    h = _modnorm(x_ref[...], m[0:1], m[1:2]).astype(BF16)
    w = q_ref.shape[1]
    pq = _dot(h, w_ref[:, 0:w])
    pk = _dot(h, w_ref[:, w:2 * w])
    if has_rope:
        reps = w // LANES
        cos = jnp.concatenate([cos_ref[...]] * reps, axis=1)
        sa = jnp.concatenate([sa_ref[...]] * reps, axis=1)
        sb = jnp.concatenate([sb_ref[...]] * reps, axis=1)
        half = DA_QK_DIM // 4

        def rope(p):
            return p * cos + pltpu.roll(p, w - half, axis=1) * sa + pltpu.roll(p, half, axis=1) * sb

        pq, pk = rope(pq), rope(pk)
    q_ref[...] = (pq * (DA_QK_DIM ** -0.5 * LOG2E)).astype(BF16)
    k_ref[...] = pk.astype(BF16)
    vt_ref[...] = _dot_nt(wvt_ref[...], h).astype(BF16)
    u_ref[...] = _dot(h, w_ref[:, 2 * w:3 * w])


def _inproj_even(xs, mods, w_qku, w_vt, rope_tabs, tm, seq_len, mod_row, name):
    rows, d = xs.shape
    w = w_vt.shape[0]
    has_rope = rope_tabs is not None
    in_specs = [pl.BlockSpec((tm, d), lambda i: (i, 0)), _mod_spec(d, tm, seq_len, mod_row),
                _resident(w_qku.shape), _resident(w_vt.shape)]
    args = [xs, mods, w_qku, w_vt]
    if has_rope:
        tiles_per_seq = seq_len // tm
        in_specs += [pl.BlockSpec((tm, LANES), lambda i: (i % tiles_per_seq, 0))] * 3
        args += list(rope_tabs)
    tok = lambda dt: jax.ShapeDtypeStruct((rows, w), dt)
    tok_spec = pl.BlockSpec((tm, w), lambda i: (i, 0))
    return pl.pallas_call(
        functools.partial(_inproj_even_kernel, has_rope=has_rope),
        grid=(rows // tm,),
        in_specs=in_specs,
        out_specs=[tok_spec, tok_spec, pl.BlockSpec((w, tm), lambda i: (0, i)), tok_spec],
        out_shape=[tok(BF16), tok(BF16), jax.ShapeDtypeStruct((w, rows), BF16), tok(F32)],
        compiler_params=_params(1),
        name=name,
    )(*args)


def _rope_tables(seq):
    rows_n = seq // GRID_W
    row = np.repeat(np.arange(rows_n, dtype=np.float64), GRID_W)
    col = np.tile(np.arange(GRID_W, dtype=np.float64), rows_n)
    n_freq = DA_QK_DIM // 4
    inv = ROPE_THETA ** (-np.arange(n_freq, dtype=np.float64) / n_freq)
    ang_r, ang_c = row[:, None] * inv, col[:, None] * inv
    zeros = np.zeros_like(ang_r)
    cos64 = np.concatenate([np.cos(ang_r)] * 2 + [np.cos(ang_c)] * 2, axis=1)
    sa64 = np.concatenate([-np.sin(ang_r), zeros, -np.sin(ang_c), zeros], axis=1)
    sb64 = np.concatenate([zeros, np.sin(ang_r), zeros, np.sin(ang_c)], axis=1)
    return tuple(jnp.asarray(np.concatenate([t, t], axis=1), dtype=F32) for t in (cos64, sa64, sb64))


def _attn_kernel(*refs, lambda_init, has_lat):
    if has_lat:
        lqk_ref, gain_ref, q_ref, kc_ref, vtc_ref, kl_ref, vtl_ref, o_ref, s_ref, vta_ref = refs
    else:
        lqk_ref, gain_ref, q_ref, kc_ref, vtc_ref, o_ref, s_ref, vta_ref = refs
    tq = s_ref.shape[3]
    nq = q_ref.shape[0] // tq
    n_ctx = kc_ref.shape[0]
    dv = vtc_ref.shape[0]
    vta_ref[0:dv, 0:n_ctx] = vtc_ref[...]
    if has_lat:
        vta_ref[0:dv, n_ctx:] = vtl_ref[...]
    ones_row = lax.broadcasted_iota(jnp.int32, (vta_ref.shape[0] - dv, vta_ref.shape[1]), 0) == 0
    vta_ref[dv:, :] = jnp.where(ones_row, 1.0, 0.0).astype(BF16)
    a = lqk_ref[...]
    lam = (jnp.exp(jnp.sum(a[0:1] * a[1:2], axis=-1, keepdims=True))
           - jnp.exp(jnp.sum(a[2:3] * a[3:4], axis=-1, keepdims=True)) + lambda_init)
    lane = lax.broadcasted_iota(jnp.int32, (1, LANES), 1)
    map_lanes = [lane < DA_QK_DIM, lane >= DA_QK_DIM]
    gain = gain_ref[...] * (1.0 - lambda_init)

    def tile_rows(t):
        return pl.ds(t * tq, tq)

    def scores(t, slot):
        q = q_ref[tile_rows(t), :]
        for i, sel in enumerate(map_lanes):
            qm = jnp.where(sel, q, jnp.zeros_like(q))
            s_ref[slot, i, 0:n_ctx, :] = _dot_nt(kc_ref[...], qm)
            if has_lat:
                s_ref[slot, i, n_ctx:, :] = _dot_nt(kl_ref[...], qm)

    def softmax_pv(t, slot):
        outs = []
        for i in range(2):
            s = s_ref[slot, i]
            e = jnp.exp2(s - jnp.max(s, axis=0, keepdims=True)).astype(BF16)
            oa = _dot(vta_ref[...], e)
            outs.append(oa[0:dv] / oa[dv:dv + 1])
        ot = outs[0] - lam * outs[1]
        ms = jnp.mean(ot * ot, axis=0, keepdims=True)
        o_ref[tile_rows(t), :] = ((ot * lax.rsqrt(ms + EPS)).T * gain).astype(BF16)

    n_slots = s_ref.shape[0]
    for t in range(min(n_slots - 1, nq)):
        scores(t, t % n_slots)
    for t in range(nq):
        if t + n_slots - 1 < nq:
            scores(t + n_slots - 1, (t + n_slots - 1) % n_slots)
        softmax_pv(t, t % n_slots)


def _diff_attention(q_lat, k_lat, vt_lat, q_ctx, k_ctx, vt_ctx, lqk, subln, lambda_init, batch, seq, ctx_len):
    w = q_lat.shape[1]
    assert (seq // TQ) % 2 == 0
    gain = subln.reshape(1, w)
    small = [pl.BlockSpec(lqk.shape, lambda b, h: (0, 0)),
             pl.BlockSpec((1, LANES), lambda b, h: (0, h))]
    ctx_k = pl.BlockSpec((ctx_len, LANES), lambda b, h: (b, h))
    ctx_vt = pl.BlockSpec((LANES, ctx_len), lambda b, h: (h, b))
    lat_k = pl.BlockSpec((seq, LANES), lambda b, h: (b, h))
    lat_vt = pl.BlockSpec((LANES, seq), lambda b, h: (h, b))
    o_lat = pl.pallas_call(
        functools.partial(_attn_kernel, lambda_init=lambda_init, has_lat=True),
        grid=(batch, DA_HEADS),
        in_specs=small + [lat_k, ctx_k, ctx_vt, lat_k, lat_vt],
        out_specs=lat_k,
        out_shape=jax.ShapeDtypeStruct(q_lat.shape, BF16),
        scratch_shapes=[pltpu.VMEM((ATTN_SLOTS, 2, ctx_len + seq, TQ), F32),
                        pltpu.VMEM((LANES + 2 * SUBLANES, ctx_len + seq), BF16)],
        compiler_params=_params(2),
        name="diff_attn_latent",
    )(lqk, gain, q_lat, k_ctx, vt_ctx, k_lat, vt_lat)
    o_ctx = pl.pallas_call(
        functools.partial(_attn_kernel, lambda_init=lambda_init, has_lat=False),
        grid=(batch, DA_HEADS),
        in_specs=small + [ctx_k, ctx_k, ctx_vt],
        out_specs=ctx_k,
        out_shape=jax.ShapeDtypeStruct(q_ctx.shape, BF16),
        scratch_shapes=[pltpu.VMEM((1, 2, ctx_len, ctx_len), F32),
                        pltpu.VMEM((LANES + 2 * SUBLANES, ctx_len), BF16)],
        compiler_params=_params(2),
        name="diff_attn_context",
    )(lqk, gain, q_ctx, k_ctx, vt_ctx)
    return o_lat, o_ctx


def _pool_kernel(u_ref, pw_ref, ps_ref, out_ref):
    n = u_ref.shape[0]
    t = lax.broadcasted_iota(jnp.int32, (n, 1), 0)
    pad = max(POOL_WINDOWS) // 2
    n_pad = n + 2 * pad
    zeros = jnp.zeros((pad, LANES), F32)
    for g, w in enumerate(POOL_WINDOWS):
        sl = slice(g * LANES, (g + 1) * LANES)
        ug = u_ref[:, sl]
        lo, hi = -(w // 2), w - w // 2
        p = jnp.concatenate([zeros, ug, zeros], axis=0)
        p = pltpu.roll(p, 1, axis=0) + p
        width = 2
        while width < w:
            p = pltpu.roll(p, width // 2, axis=0) + pltpu.roll(p, n_pad - width // 2, axis=0)
            width *= 2
        cnt = (jnp.minimum(t + hi, n) - jnp.maximum(t + lo, 0)).astype(F32)
        dlt = p[pad:pad + n] / cnt - ug
        y = _dot(dlt.astype(BF16), pw_ref[g]) * ps_ref[:, sl]
        out_ref[:, sl] = y.astype(BF16)


def _pool(u, pool_w, pool_scale, seq_len, name):
    rows, w = u.shape
    return pl.pallas_call(
        _pool_kernel,
        grid=(rows // seq_len,),
        in_specs=[pl.BlockSpec((seq_len, w), lambda b: (b, 0)),
                  pl.BlockSpec(pool_w.shape, lambda b: (0, 0, 0)),
                  pl.BlockSpec((1, w), lambda b: (0, 0))],
        out_specs=pl.BlockSpec((seq_len, w), lambda b: (b, 0)),
        out_shape=jax.ShapeDtypeStruct((rows, w), BF16),
        compiler_params=_params(1),
        name=name,
    )(u, pool_w, pool_scale.reshape(1, w))


def _ff_chunks(d_ff):
    tiles = d_ff // MXU_DIM
    assert tiles * MXU_DIM == d_ff
    n_chunks = -(-tiles // (FF_CHUNK // MXU_DIM))
    sizes = [(tiles // n_chunks + (j < tiles % n_chunks)) * MXU_DIM for j in range(n_chunks)]
    return [(sum(sizes[:j]), sizes[j]) for j in range(n_chunks)]


def _halo_specs(tm, width, rows, halo):
    tb = tm // halo
    last = rows // halo - 1
    return [pl.BlockSpec((tm, width), lambda i: (i, 0)),
            pl.BlockSpec((halo, width), lambda i: (jnp.maximum(i * tb - 1, 0), 0)),
            pl.BlockSpec((halo, width), lambda i: (jnp.minimum((i + 1) * tb, last), 0))]


def _with_halo(ref, prev_ref, next_ref):
    n = prev_ref.shape[0]
    prev = prev_ref[...].astype(F32)[n - SUBLANES:n]
    nxt = next_ref[...].astype(F32)[0:SUBLANES]
    return jnp.concatenate([prev, ref[...].astype(F32), nxt], axis=0)


def _layer_tail_kernel(*refs, mixer, tiles_per_seq, final):
    if mixer == "even":
        (x_ref, xp_ref, xn_ref, o_ref, op_ref, on_ref, p_ref, pp_ref, pn_ref, mod_ref, wo_ref,
         wu_ref, cw_ref, cb_ref, wd_ref, fin_ref, out_ref) = refs
    else:
        (x_ref, xp_ref, xn_ref, o_ref, op_ref, on_ref, g_ref, gp_ref, gn_ref, gain_ref, mod_ref, wo_ref,
         wu_ref, cw_ref, cb_ref, wd_ref, fin_ref, out_ref, hn_ref) = refs
    i = pl.program_id(0)
    tm = x_ref.shape[0]
    d_ff = wd_ref.shape[0]
    ext = tm + 2 * SUBLANES
    at_start = i % tiles_per_seq == 0
    at_end = (i + 1) % tiles_per_seq == 0
    m = mod_ref[0]

    if mixer == "even":
        k = o_ref.shape[1]
        pad = op_ref.shape[0] - SUBLANES
        o_ext = jnp.concatenate([op_ref[...], o_ref[...], on_ref[...]], axis=0)
        p_ext = jnp.concatenate([pp_ref[...], p_ref[...], pn_ref[...]], axis=0)
        y = (_dot(o_ext, wo_ref[0:k, :]) + _dot(p_ext, wo_ref[k:, :]))[pad:pad + ext]
    else:
        o_ext = _with_halo(o_ref, op_ref, on_ref)
        g_ext = _with_halo(g_ref, gp_ref, gn_ref)
        for h in range(o_ext.shape[1] // HG_HEAD_DIM):
            sl = slice(h * HG_HEAD_DIM, (h + 1) * HG_HEAD_DIM)
            o, g = o_ext[:, sl], g_ext[:, sl]
            ms = jnp.mean(o * o, axis=-1, keepdims=True)
            hn_ref[:, sl] = (o * lax.rsqrt(ms + EPS) * gain_ref[:, sl] * (g * _sigmoid(g))).astype(BF16)
        y = _dot(hn_ref[...], wo_ref[...])
    x1 = _with_halo(x_ref, xp_ref, xn_ref) + m[2:3] * y

    h = _modnorm(x1, m[3:4], m[4:5])
    h = jnp.concatenate([jnp.where(at_start, 0.0, h[0:SUBLANES]), h[SUBLANES:SUBLANES + tm],
                         jnp.where(at_end, 0.0, h[SUBLANES + tm:])], axis=0).astype(BF16)

    def conv(c0, n):
        u = _dot(h, wu_ref[:, c0:c0 + n])
        cw = cw_ref[:, c0:c0 + n]
        y = (pltpu.roll(u, 1, axis=0) * cw[0:1] + u * cw[1:2] + pltpu.roll(u, ext - 1, axis=0) * cw[2:3]
             + cb_ref[:, c0:c0 + n])
        return y[SUBLANES:SUBLANES + tm]

    acc = jnp.zeros((tm, out_ref.shape[1]), F32)
    for c0, n in _ff_chunks(d_ff):
        a = conv(c0, n)
        g = conv(d_ff + c0, n)
        act = (a * (g * _sigmoid(g))).astype(BF16)
        acc = acc + _dot(act, wd_ref[c0:c0 + n, :])
    y = x1[SUBLANES:SUBLANES + tm] + m[5:6] * acc
    if final:
        ms = jnp.mean(y * y, axis=-1, keepdims=True)
        y = y * lax.rsqrt(ms + EPS) * fin_ref[...]
    out_ref[...] = y


def _layer_tail(xs, mods, mixer, mix_in, w_out, ffn_w, tm, seq_len, mod_row, final, name):
    rows, d = xs.shape
    w_up, conv_w, conv_b, w_down, final_norm = ffn_w
    bf16_rows = 2 * SUBLANES
    in_specs = _halo_specs(tm, d, rows, SUBLANES)
    args = [xs] * 3
    scratch = []
    if mixer == "even":
        o, p = mix_in
        in_specs += _halo_specs(tm, o.shape[1], rows, bf16_rows) + _halo_specs(tm, p.shape[1], rows, bf16_rows)
        args += [o] * 3 + [p] * 3
    else:
        o, g, gain = mix_in
        w = o.shape[1]
        in_specs += _halo_specs(tm, w, rows, SUBLANES) + _halo_specs(tm, w, rows, bf16_rows) + [_resident((1, w))]
        args += [o] * 3 + [g] * 3 + [gain.reshape(1, w)]
        scratch = [pltpu.VMEM((tm + 2 * SUBLANES, w), BF16)]
    in_specs += [_mod_spec(d, tm, seq_len, mod_row), _resident(w_out.shape),
                 _resident(w_up.shape), _resident(conv_w.shape), _resident((1, conv_b.shape[0])),
                 _resident(w_down.shape), _resident((1, d))]
    args += [mods, w_out, w_up, conv_w, conv_b.reshape(1, -1), w_down, final_norm.reshape(1, d)]
    return pl.pallas_call(
        functools.partial(_layer_tail_kernel, mixer=mixer, tiles_per_seq=seq_len // tm, final=final),
        grid=(rows // tm,),
        in_specs=in_specs,
        out_specs=pl.BlockSpec((tm, d), lambda i: (i, 0)),
        out_shape=jax.ShapeDtypeStruct((rows, d), F32),
        scratch_shapes=scratch,
        compiler_params=_params(1),
        name=name,
    )(*args)


def _inproj_odd_kernel(x_ref, mod_ref, w_ref, *out_refs):
    m = mod_ref[0]
    h = _modnorm(x_ref[...], m[0:1], m[1:2]).astype(BF16)
    w = out_refs[0].shape[1]
    step = 512
    for c0 in range(0, w, step):
        sl = slice(c0, c0 + step)
        for n, ref in enumerate(out_refs):
            p = _dot(h, w_ref[:, n * w + c0:n * w + c0 + step])
            if n == 0:
                p = p * _sigmoid(p)
            ref[:, sl] = p.astype(ref.dtype)


def _inproj_odd(xs, mods, w_in, tm, seq_len, mod_row, want_gate, name):
    rows, d = xs.shape
    w = w_in.shape[1] // 5
    dtypes = [BF16, BF16, F32, F32] + ([BF16] if want_gate else [])
    return pl.pallas_call(
        _inproj_odd_kernel,
        grid=(rows // tm,),
        in_specs=[pl.BlockSpec((tm, d), lambda i: (i, 0)), _mod_spec(d, tm, seq_len, mod_row),
                  _resident(w_in.shape)],
        out_specs=[pl.BlockSpec((tm, w), lambda i: (i, 0))] * len(dtypes),
        out_shape=[jax.ShapeDtypeStruct((rows, w), dt) for dt in dtypes],
        compiler_params=_params(1),
        name=name,
    )(xs, mods, w_in)


def _cumsum_rows(tri, x):
    w = x.shape[1]
    hi = x.astype(BF16)
    lo = (x - hi.astype(F32)).astype(BF16)
    s = _dot(tri, jnp.concatenate([hi, lo], axis=1))
    return s[:, 0:w] + s[:, w:2 * w]


def _hgrn2_kernel(lbl_ref, qc_ref, vc_ref, zfc_ref, zbc_ref, ql_ref, vl_ref, zfl_ref, zbl_ref,
                  o_ref, ob_ref, st_ref, *, layer):
    c = HG_CHUNK
    rb = HG_BLOCK
    n_sub = rb // c
    n_levels = n_sub.bit_length() - 1
    w = o_ref.shape[1]
    heads = w // HG_HEAD_DIM
    n_ctx = qc_ref.shape[0] // rb
    n_lat = ql_ref.shape[0] // rb
    r_i = lax.broadcasted_iota(jnp.int32, (rb, rb), 0)
    c_i = lax.broadcasted_iota(jnp.int32, (rb, rb), 1)
    same_group = [(r_i // (c << l)) == (c_i // (c << l)) for l in range(n_levels)]
    chunk_id = lax.broadcasted_iota(jnp.int32, (n_sub, 1, w), 0)
    hsl = [slice(h * HG_HEAD_DIM, (h + 1) * HG_HEAD_DIM) for h in range(heads)]

    def direction_constants(d):
        fwd = d == 0
        lg = lbl_ref[d]
        e = jnp.exp(lg - jnp.max(lg, axis=0, keepdims=True))
        p = e / jnp.sum(e, axis=0, keepdims=True)
        lb = jnp.sum(p[1:layer + 1], axis=0, keepdims=True)
        ordered = (c_i <= r_i) if fwd else (c_i >= r_i)
        tri = jnp.where(ordered, 1.0, 0.0).astype(BF16)
        return lb, same_group[0] & ordered, tri

    consts = [direction_constants(0), direction_constants(1)]

    def block(d, q, v, z, want_out, out):
        fwd = d == 0
        lb, causal, tri = consts[d]
        split = lambda a, n: a.reshape(rb // n, n, w)
        merge = lambda a: a.reshape(rb, w).astype(BF16)
        f = lb + (1.0 - lb) * _sigmoid(z)
        k = 1.0 - f
        logf = jnp.log(f)
        yield
        bg = _cumsum_rows(tri, logf)
        yield
        be = bg[rb - 1:rb] if fwd else bg[0:1]
        decay = jnp.exp(be)
        if want_out:
            b0 = split(bg, c)
            r0 = b0[:, c // 2 - 1:c // 2, :] if fwd else b0[:, c // 2:c // 2 + 1, :]
            qt = split(q.astype(F32), c) * jnp.exp(b0 - r0)
            kt = split(k, c) * jnp.exp(r0 - b0)
            q_lv, k_lv = [merge(qt)], [merge(kt)]
            for l in range(1, n_levels + 1):
                g = c << l
                bl = split(bg, g)
                m = bl[:, g // 2 - 1:g // 2, :] if fwd else bl[:, g // 2:g // 2 + 1, :]
                m = jnp.broadcast_to(m[:, None], (rb // g, 1 << l, 1, w)).reshape(n_sub, 1, w)
                late_half = ((chunk_id >> (l - 1)) & 1) == 1
                q_side = late_half if fwd else ~late_half
                q_lv.append(merge(qt * jnp.where(q_side, jnp.exp(r0 - m), 0.0)))
                k_lv.append(merge(kt * jnp.where(q_side, 0.0, jnp.exp(m - r0))))
            qh = merge(qt * jnp.exp(r0))
            kh = merge(kt * jnp.exp(be - r0))
        else:
            kh = (k * jnp.exp(be - bg)).astype(BF16)
        yield
        ds = [_dot_tn(v[:, sl], kh[:, sl]) for sl in hsl]
        if want_out:
            a_lv = [[_dot_nt(ql[:, sl], kl[:, sl]) for ql, kl in zip(q_lv, k_lv)] for sl in hsl]
        yield
        st_old = []
        for h in range(heads):
            st = st_ref[d, h]
            st_old.append(st.astype(BF16))
            st_ref[d, h] = st * decay[:, hsl[h]] + ds[h]
        if want_out:
            a = []
            for lv in a_lv:
                x = lv[n_levels]
                for l in range(n_levels - 1, 0, -1):
                    x = jnp.where(same_group[l], lv[l], x)
                a.append(jnp.where(causal, lv[0], x).astype(BF16))
        yield
        if want_out:
            out[d] = jnp.concatenate([_dot(a[h], v[:, hsl[h]]) + _dot_nt(qh[:, hsl[h]], st_old[h])
                                      for h in range(heads)], axis=1)
        yield

    def run_interleaved(gens):
        for _ in zip(*gens):
            pass

    def block_rows(d, j, n):
        return pl.ds(pl.multiple_of((j if d == 0 else n - 1 - j) * rb, rb), rb)

    def ctx_step(j, carry):
        gens = []
        for d, z_ref in ((0, zfc_ref), (1, zbc_ref)):
            rows = block_rows(d, j, n_ctx)
            gens.append(block(d, qc_ref[rows, :], vc_ref[rows, :], z_ref[rows, :], False, None))
        run_interleaved(gens)
        return carry

    def lat_step(j, carry):
        out = {}
        rows = [block_rows(d, j, n_lat) for d in (0, 1)]
        gens = [block(d, ql_ref[rows[d], :], vl_ref[rows[d], :], z_ref[rows[d], :], True, out)
                for d, z_ref in ((0, zfl_ref), (1, zbl_ref))]
        run_interleaved(gens)
        o_ref[rows[0], :] = out[0]
        ob_ref[rows[1], :] = out[1]
        return carry

    st_ref[...] = jnp.zeros_like(st_ref)
    lax.fori_loop(0, n_ctx, ctx_step, 0)
    lax.fori_loop(0, n_lat, lat_step, 0, unroll=2)
    o_ref[...] += ob_ref[...]


def _hgrn2(lat_in, ctx_in, lb_logits, layer, batch, seq, ctx_len):
    w = lat_in[0].shape[1]
    ctx = pl.BlockSpec((ctx_len, HG_LANES), lambda b, g: (b, g))
    lat = pl.BlockSpec((seq, HG_LANES), lambda b, g: (b, g))
    heads = HG_LANES // HG_HEAD_DIM
    return pl.pallas_call(
        functools.partial(_hgrn2_kernel, layer=layer),
        grid=(batch, w // HG_LANES),
        in_specs=[pl.BlockSpec((2, lb_logits.shape[1], HG_LANES), lambda b, g: (0, 0, g))] + [ctx] * 4 + [lat] * 4,
        out_specs=lat,
        out_shape=jax.ShapeDtypeStruct((batch * seq, w), F32),
        scratch_shapes=[pltpu.VMEM((seq, HG_LANES), F32),
                        pltpu.VMEM((2, heads, HG_HEAD_DIM, HG_HEAD_DIM), F32)],
        compiler_params=_params(2),
        name="hgrn2_scan",
    )(lb_logits, *ctx_in, *lat_in)


def kernel(x, c, ctx, c_ctx, w_mod, b_mod, ev_w_in, ev_w_out, da_lq1, da_lk1, da_lq2, da_lk2, da_subln, pool_w,
           pool_scale, hg_w_in, hg_w_out, hg_lb_logits, hg_norm, ffn_w_up, ffn_conv_w, ffn_conv_b, ffn_w_down,
           final_norm):
    batch, seq, d = x.shape
    ctx_len = ctx.shape[1]
    depth = w_mod.shape[0]
    assert seq % TM_LAT == 0 and seq % TQ == 0 and seq % HG_BLOCK == 0
    assert ctx_len % TM_CTX == 0 and ctx_len % HG_BLOCK == 0

    xl = x.reshape(batch * seq, d)
    xc = ctx.reshape(batch * ctx_len, d)
    cond_rows = 2 * SUBLANES
    cond = jnp.zeros((cond_rows, d), F32).at[:batch].set(c).at[batch].set(c_ctx)
    mods_all = _modulation(cond, w_mod, b_mod).reshape(depth, cond_rows, 6, d)
    lat = dict(tm=TM_LAT, seq_len=seq, mod_row=None)
    cxt = dict(tm=TM_CTX, seq_len=ctx_len, mod_row=batch)

    for i in range(depth):
        last = i == depth - 1
        mods = mods_all[i]
        j = i // 2
        ffn_w = (ffn_w_up[i].astype(BF16), ffn_conv_w[i], ffn_conv_b[i], ffn_w_down[i].astype(BF16), final_norm)
        if i % 2 == 0:
            lambda_init = 0.8 - 0.6 * math.exp(-0.3 * i)
            wq = ev_w_in[j].shape[1] // 4
            w_qku = jnp.concatenate([ev_w_in[j][:, :2 * wq], ev_w_in[j][:, 3 * wq:]], axis=1).astype(BF16)
            w_vt = ev_w_in[j][:, 2 * wq:3 * wq].T.astype(BF16)
            w_out = ev_w_out[j].astype(BF16)
            pw = pool_w[j].astype(BF16)
            lqk = jnp.stack([da_lq1[j], da_lk1[j], da_lq2[j], da_lk2[j]]).astype(F32)
            ql, kl, vtl, ul = _inproj_even(xl, mods, w_qku, w_vt, _rope_tables(seq), name="even_inproj_latent", **lat)
            qc, kc, vtc, uc = _inproj_even(xc, mods, w_qku, w_vt, None, name="even_inproj_context", **cxt)
            ol, oc = _diff_attention(ql, kl, vtl, qc, kc, vtc, lqk, da_subln[j], lambda_init, batch, seq, ctx_len)
            mix_lat = (ol, _pool(ul, pw, pool_scale[j], seq, "pool_latent"))
            mix_ctx = (oc, _pool(uc, pw, pool_scale[j], ctx_len, "pool_context"))
            mixer = "even"
        else:
            if not last:
                raise NotImplementedError("context readout of an HGRN2 layer that is not the last layer")
            w_in = hg_w_in[j].astype(BF16)
            *lat_in, gl = _inproj_odd(xl, mods, w_in, want_gate=True, name="odd_inproj_latent", **lat)
            ctx_in = _inproj_odd(xc, mods, w_in, want_gate=False, name="odd_inproj_context", **cxt)
            o = _hgrn2(lat_in, ctx_in, hg_lb_logits.astype(F32), i, batch, seq, ctx_len)
            mix_lat, mix_ctx = (o, gl, hg_norm[j]), None
            w_out = hg_w_out[j].astype(BF16)
            mixer = "odd"
        xl = _layer_tail(xl, mods, mixer, mix_lat, w_out, ffn_w, final=last, name="layer_tail_latent", **lat)
        if not last:
            xc = _layer_tail(xc, mods, mixer, mix_ctx, w_out, ffn_w, final=False, name="layer_tail_context", **cxt)
    return xl.reshape(batch, seq, d)
```

```python
import functools
import math

import jax
import jax.numpy as jnp
import numpy as np
from jax import lax
from jax.experimental import pallas as pl
from jax.experimental.pallas import tpu as pltpu

F32 = jnp.float32
BF16 = jnp.bfloat16

EPS = 1e-6
GRID_W = 64
DA_HEADS = 4
DA_QK_DIM = 64
ROPE_THETA = 10000.0
POOL_WINDOWS = (2, 4, 8, 16)
HG_HEAD_DIM = 128
LANES = 128
SUBLANES = 8
VMEM_LIMIT = 56 * 1024 * 1024

TM_LAT = 512
TM_PROJ = 1024
TM_CTX = 256
TQ = 256
ATTN_SLOTS = 3
HG_CHUNK = 32
HG_LANES = 512
HG_BLOCK = 256
MXU_DIM = 256
FF_CHUNK = 1536
LOG2E = math.log2(math.e)


def _resident(shape):
    nd = len(shape)
    return pl.BlockSpec(shape, lambda *_: (0,) * nd, pipeline_mode=pl.Buffered(1))


def _params(n_axes):
    return pltpu.CompilerParams(dimension_semantics=("parallel",) * n_axes, vmem_limit_bytes=VMEM_LIMIT)


def _mod_spec(d, tm, seq_len, fixed_row):
    if fixed_row is None:
        tiles_per_seq = seq_len // tm
        return pl.BlockSpec((1, 6, d), lambda i: (i // tiles_per_seq, 0, 0))
    return pl.BlockSpec((1, 6, d), lambda i: (fixed_row, 0, 0))


def _sigmoid(x):
    return 1.0 / (1.0 + jnp.exp(-x))


def _modnorm(x, shift, scale):
    ms = jnp.mean(x * x, axis=-1, keepdims=True)
    return (x * lax.rsqrt(ms + EPS)) * (1.0 + scale) + shift


def _dot(a, b):
    return jnp.dot(a, b, preferred_element_type=F32)


def _dot_nt(a, b):
    return lax.dot_general(a, b, (((1,), (1,)), ((), ())), preferred_element_type=F32)


def _dot_tn(a, b):
    return lax.dot_general(a, b, (((0,), (0,)), ((), ())), preferred_element_type=F32)


def _mod_kernel(cond_ref, w_ref, b_ref, out_ref):
    c = cond_ref[...]
    rows = c.shape[0]
    s = c * _sigmoid(c)
    hi = s.astype(BF16)
    lo = (s - hi.astype(F32)).astype(BF16)
    y = _dot(jnp.concatenate([hi, lo], axis=0), w_ref[0].astype(BF16))
    out_ref[0] = y[0:rows] + y[rows:] + b_ref[0]


def _modulation(cond, w_mod, b_mod):
    depth, d, n = w_mod.shape
    rows = cond.shape[0]
    tn = 2048
    return pl.pallas_call(
        _mod_kernel,
        grid=(depth, n // tn),
        in_specs=[pl.BlockSpec((rows, d), lambda l, j: (0, 0)),
                  pl.BlockSpec((1, d, tn), lambda l, j: (l, 0, j)),
                  pl.BlockSpec((1, 1, tn), lambda l, j: (l, 0, j))],
        out_specs=pl.BlockSpec((1, rows, tn), lambda l, j: (l, 0, j)),
        out_shape=jax.ShapeDtypeStruct((depth, rows, n), F32),
        compiler_params=_params(2),
        name="adaln_mod",
    )(cond, w_mod, b_mod.reshape(depth, 1, n))


def _inproj_even_kernel(*refs, has_rope):
    if has_rope:
        x_ref, mod_ref, w_ref, wvt_ref, cos_ref, sa_ref, sb_ref, q_ref, k_ref, vt_ref, u_ref = refs
    else:
        x_ref, mod_ref, w_ref, wvt_ref, q_ref, k_ref, vt_ref, u_ref = refs
    m = mod_ref[0]
    h = _modnorm(x_ref[...], m[0:1], m[1:2]).astype(BF16)
    w = q_ref.shape[1]
    pq = _dot(h, w_ref[:, 0:w])
    pk = _dot(h, w_ref[:, w:2 * w])
    if has_rope:
        reps = w // LANES
        cos = jnp.concatenate([cos_ref[...]] * reps, axis=1)
        sa = jnp.concatenate([sa_ref[...]] * reps, axis=1)
        sb = jnp.concatenate([sb_ref[...]] * reps, axis=1)
        half = DA_QK_DIM // 4

        def rope(p):
            return p * cos + pltpu.roll(p, w - half, axis=1) * sa + pltpu.roll(p, half, axis=1) * sb

        pq, pk = rope(pq), rope(pk)
    q_ref[...] = (pq * (DA_QK_DIM ** -0.5 * LOG2E)).astype(BF16)
    k_ref[...] = pk.astype(BF16)
    vt_ref[...] = _dot_nt(wvt_ref[...], h).astype(BF16)
    u_ref[...] = _dot(h, w_ref[:, 2 * w:3 * w])


def _inproj_even(xs, mods, w_qku, w_vt, rope_tabs, tm, seq_len, mod_row, name):
    rows, d = xs.shape
    w = w_vt.shape[0]
    has_rope = rope_tabs is not None
    in_specs = [pl.BlockSpec((tm, d), lambda i: (i, 0)), _mod_spec(d, tm, seq_len, mod_row),
                _resident(w_qku.shape), _resident(w_vt.shape)]
    args = [xs, mods, w_qku, w_vt]
    if has_rope:
        tiles_per_seq = seq_len // tm
        in_specs += [pl.BlockSpec((tm, LANES), lambda i: (i % tiles_per_seq, 0))] * 3
        args += list(rope_tabs)
    tok = lambda dt: jax.ShapeDtypeStruct((rows, w), dt)
    tok_spec = pl.BlockSpec((tm, w), lambda i: (i, 0))
    return pl.pallas_call(
        functools.partial(_inproj_even_kernel, has_rope=has_rope),
        grid=(rows // tm,),
        in_specs=in_specs,
        out_specs=[tok_spec, tok_spec, pl.BlockSpec((w, tm), lambda i: (0, i)), tok_spec],
        out_shape=[tok(BF16), tok(BF16), jax.ShapeDtypeStruct((w, rows), BF16), tok(F32)],
        compiler_params=_params(1),
        name=name,
    )(*args)


def _rope_tables(seq):
    rows_n = seq // GRID_W
    row = np.repeat(np.arange(rows_n, dtype=np.float64), GRID_W)
    col = np.tile(np.arange(GRID_W, dtype=np.float64), rows_n)
    n_freq = DA_QK_DIM // 4
    inv = ROPE_THETA ** (-np.arange(n_freq, dtype=np.float64) / n_freq)
    ang_r, ang_c = row[:, None] * inv, col[:, None] * inv
    zeros = np.zeros_like(ang_r)
    cos64 = np.concatenate([np.cos(ang_r)] * 2 + [np.cos(ang_c)] * 2, axis=1)
    sa64 = np.concatenate([-np.sin(ang_r), zeros, -np.sin(ang_c), zeros], axis=1)
    sb64 = np.concatenate([zeros, np.sin(ang_r), zeros, np.sin(ang_c)], axis=1)
    return tuple(jnp.asarray(np.concatenate([t, t], axis=1), dtype=F32) for t in (cos64, sa64, sb64))


def _attn_kernel(*refs, lambda_init, has_lat):
    if has_lat:
        lqk_ref, gain_ref, q_ref, kc_ref, vtc_ref, kl_ref, vtl_ref, o_ref, s_ref, vta_ref = refs
    else:
        lqk_ref, gain_ref, q_ref, kc_ref, vtc_ref, o_ref, s_ref, vta_ref = refs
    tq = s_ref.shape[3]
    nq = q_ref.shape[0] // tq
    n_ctx = kc_ref.shape[0]
    dv = vtc_ref.shape[0]
    vta_ref[0:dv, 0:n_ctx] = vtc_ref[...]
    if has_lat:
        vta_ref[0:dv, n_ctx:] = vtl_ref[...]
    ones_row = lax.broadcasted_iota(jnp.int32, (vta_ref.shape[0] - dv, vta_ref.shape[1]), 0) == 0
    vta_ref[dv:, :] = jnp.where(ones_row, 1.0, 0.0).astype(BF16)
    a = lqk_ref[...]
    lam = (jnp.exp(jnp.sum(a[0:1] * a[1:2], axis=-1, keepdims=True))
           - jnp.exp(jnp.sum(a[2:3] * a[3:4], axis=-1, keepdims=True)) + lambda_init)
    lane = lax.broadcasted_iota(jnp.int32, (1, LANES), 1)
    map_lanes = [lane < DA_QK_DIM, lane >= DA_QK_DIM]
    gain = gain_ref[...] * (1.0 - lambda_init)

    def tile_rows(t):
        return pl.ds(t * tq, tq)

    def scores(t, slot):
        q = q_ref[tile_rows(t), :]
        for i, sel in enumerate(map_lanes):
            qm = jnp.where(sel, q, jnp.zeros_like(q))
            s_ref[slot, i, 0:n_ctx, :] = _dot_nt(kc_ref[...], qm)
            if has_lat:
                s_ref[slot, i, n_ctx:, :] = _dot_nt(kl_ref[...], qm)

    def softmax_pv(t, slot):
        outs = []
        for i in range(2):
            s = s_ref[slot, i]
            e = jnp.exp2(s - jnp.max(s, axis=0, keepdims=True)).astype(BF16)
            oa = _dot(vta_ref[...], e)
            outs.append(oa[0:dv] / oa[dv:dv + 1])
        ot = outs[0] - lam * outs[1]
        ms = jnp.mean(ot * ot, axis=0, keepdims=True)
        o_ref[tile_rows(t), :] = ((ot * lax.rsqrt(ms + EPS)).T * gain).astype(BF16)

    n_slots = s_ref.shape[0]
    for t in range(min(n_slots - 1, nq)):
        scores(t, t % n_slots)
    for t in range(nq):
        if t + n_slots - 1 < nq:
            scores(t + n_slots - 1, (t + n_slots - 1) % n_slots)
        softmax_pv(t, t % n_slots)


def _diff_attention(q_lat, k_lat, vt_lat, q_ctx, k_ctx, vt_ctx, lqk, subln, lambda_init, batch, seq, ctx_len):
    w = q_lat.shape[1]
    assert (seq // TQ) % 2 == 0
    gain = subln.reshape(1, w)
    small = [pl.BlockSpec(lqk.shape, lambda b, h: (0, 0)),
             pl.BlockSpec((1, LANES), lambda b, h: (0, h))]
    ctx_k = pl.BlockSpec((ctx_len, LANES), lambda b, h: (b, h))
    ctx_vt = pl.BlockSpec((LANES, ctx_len), lambda b, h: (h, b))
    lat_k = pl.BlockSpec((seq, LANES), lambda b, h: (b, h))
    lat_vt = pl.BlockSpec((LANES, seq), lambda b, h: (h, b))
    o_lat = pl.pallas_call(
        functools.partial(_attn_kernel, lambda_init=lambda_init, has_lat=True),
        grid=(batch, DA_HEADS),
        in_specs=small + [lat_k, ctx_k, ctx_vt, lat_k, lat_vt],
        out_specs=lat_k,
        out_shape=jax.ShapeDtypeStruct(q_lat.shape, BF16),
        scratch_shapes=[pltpu.VMEM((ATTN_SLOTS, 2, ctx_len + seq, TQ), F32),
                        pltpu.VMEM((LANES + 2 * SUBLANES, ctx_len + seq), BF16)],
        compiler_params=_params(2),
        name="diff_attn_latent",
    )(lqk, gain, q_lat, k_ctx, vt_ctx, k_lat, vt_lat)
    o_ctx = pl.pallas_call(
        functools.partial(_attn_kernel, lambda_init=lambda_init, has_lat=False),
        grid=(batch, DA_HEADS),
        in_specs=small + [ctx_k, ctx_k, ctx_vt],
        out_specs=ctx_k,
        out_shape=jax.ShapeDtypeStruct(q_ctx.shape, BF16),
        scratch_shapes=[pltpu.VMEM((1, 2, ctx_len, ctx_len), F32),
                        pltpu.VMEM((LANES + 2 * SUBLANES, ctx_len), BF16)],
        compiler_params=_params(2),
        name="diff_attn_context",
    )(lqk, gain, q_ctx, k_ctx, vt_ctx)
    return o_lat, o_ctx


def _pool_kernel(u_ref, pw_ref, ps_ref, out_ref):
    n = u_ref.shape[0]
    t = lax.broadcasted_iota(jnp.int32, (n, 1), 0)
    pad = max(POOL_WINDOWS) // 2
    n_pad = n + 2 * pad
    zeros = jnp.zeros((pad, LANES), F32)
    for g, w in enumerate(POOL_WINDOWS):
        sl = slice(g * LANES, (g + 1) * LANES)
        ug = u_ref[:, sl]
        lo, hi = -(w // 2), w - w // 2
        p = jnp.concatenate([zeros, ug, zeros], axis=0)
        p = pltpu.roll(p, 1, axis=0) + p
        width = 2
        while width < w:
            p = pltpu.roll(p, width // 2, axis=0) + pltpu.roll(p, n_pad - width // 2, axis=0)
            width *= 2
        cnt = (jnp.minimum(t + hi, n) - jnp.maximum(t + lo, 0)).astype(F32)
        dlt = p[pad:pad + n] / cnt - ug
        y = _dot(dlt.astype(BF16), pw_ref[g]) * ps_ref[:, sl]
        out_ref[:, sl] = y.astype(BF16)


def _pool(u, pool_w, pool_scale, seq_len, name):
    rows, w = u.shape
    return pl.pallas_call(
        _pool_kernel,
        grid=(rows // seq_len,),
        in_specs=[pl.BlockSpec((seq_len, w), lambda b: (b, 0)),
                  pl.BlockSpec(pool_w.shape, lambda b: (0, 0, 0)),
                  pl.BlockSpec((1, w), lambda b: (0, 0))],
        out_specs=pl.BlockSpec((seq_len, w), lambda b: (b, 0)),
        out_shape=jax.ShapeDtypeStruct((rows, w), BF16),
        compiler_params=_params(1),
        name=name,
    )(u, pool_w, pool_scale.reshape(1, w))


def _ff_chunks(d_ff):
    tiles = d_ff // MXU_DIM
    assert tiles * MXU_DIM == d_ff
    n_chunks = -(-tiles // (FF_CHUNK // MXU_DIM))
    sizes = [(tiles // n_chunks + (j < tiles % n_chunks)) * MXU_DIM for j in range(n_chunks)]
    return [(sum(sizes[:j]), sizes[j]) for j in range(n_chunks)]


def _halo_specs(tm, width, rows, halo):
    tb = tm // halo
    last = rows // halo - 1
    return [pl.BlockSpec((tm, width), lambda i: (i, 0)),
            pl.BlockSpec((halo, width), lambda i: (jnp.maximum(i * tb - 1, 0), 0)),
            pl.BlockSpec((halo, width), lambda i: (jnp.minimum((i + 1) * tb, last), 0))]


def _with_halo(ref, prev_ref, next_ref):
    n = prev_ref.shape[0]
    prev = prev_ref[...].astype(F32)[n - SUBLANES:n]
    nxt = next_ref[...].astype(F32)[0:SUBLANES]
    return jnp.concatenate([prev, ref[...].astype(F32), nxt], axis=0)


def _layer_tail_kernel(*refs, mixer, tiles_per_seq, final):
    if mixer == "even":
        (x_ref, xp_ref, xn_ref, o_ref, op_ref, on_ref, p_ref, pp_ref, pn_ref, mod_ref, wo_ref,
         wu_ref, cw_ref, cb_ref, wd_ref, fin_ref, out_ref) = refs
    else:
        (x_ref, xp_ref, xn_ref, o_ref, op_ref, on_ref, g_ref, gp_ref, gn_ref, gain_ref, mod_ref, wo_ref,
         wu_ref, cw_ref, cb_ref, wd_ref, fin_ref, out_ref, hn_ref) = refs
    i = pl.program_id(0)
    tm = x_ref.shape[0]
    d_ff = wd_ref.shape[0]
    ext = tm + 2 * SUBLANES
    at_start = i % tiles_per_seq == 0
    at_end = (i + 1) % tiles_per_seq == 0
    m = mod_ref[0]

    if mixer == "even":
        k = o_ref.shape[1]
        pad = op_ref.shape[0] - SUBLANES
        o_ext = jnp.concatenate([op_ref[...], o_ref[...], on_ref[...]], axis=0)
        p_ext = jnp.concatenate([pp_ref[...], p_ref[...], pn_ref[...]], axis=0)
        y = (_dot(o_ext, wo_ref[0:k, :]) + _dot(p_ext, wo_ref[k:, :]))[pad:pad + ext]
    else:
        o_ext = _with_halo(o_ref, op_ref, on_ref)
        g_ext = _with_halo(g_ref, gp_ref, gn_ref)
        for h in range(o_ext.shape[1] // HG_HEAD_DIM):
            sl = slice(h * HG_HEAD_DIM, (h + 1) * HG_HEAD_DIM)
            o, g = o_ext[:, sl], g_ext[:, sl]
            ms = jnp.mean(o * o, axis=-1, keepdims=True)
            hn_ref[:, sl] = (o * lax.rsqrt(ms + EPS) * gain_ref[:, sl] * (g * _sigmoid(g))).astype(BF16)
        y = _dot(hn_ref[...], wo_ref[...])
    x1 = _with_halo(x_ref, xp_ref, xn_ref) + m[2:3] * y

    h = _modnorm(x1, m[3:4], m[4:5])
    h = jnp.concatenate([jnp.where(at_start, 0.0, h[0:SUBLANES]), h[SUBLANES:SUBLANES + tm],
                         jnp.where(at_end, 0.0, h[SUBLANES + tm:])], axis=0).astype(BF16)

    def conv(c0, n):
        u = _dot(h, wu_ref[:, c0:c0 + n])
        cw = cw_ref[:, c0:c0 + n]
        y = (pltpu.roll(u, 1, axis=0) * cw[0:1] + u * cw[1:2] + pltpu.roll(u, ext - 1, axis=0) * cw[2:3]
             + cb_ref[:, c0:c0 + n])
        return y[SUBLANES:SUBLANES + tm]

    acc = jnp.zeros((tm, out_ref.shape[1]), F32)
    for c0, n in _ff_chunks(d_ff):
        a = conv(c0, n)
        g = conv(d_ff + c0, n)
        act = (a * (g * _sigmoid(g))).astype(BF16)
        acc = acc + _dot(act, wd_ref[c0:c0 + n, :])
    y = x1[SUBLANES:SUBLANES + tm] + m[5:6] * acc
    if final:
        ms = jnp.mean(y * y, axis=-1, keepdims=True)
        y = y * lax.rsqrt(ms + EPS) * fin_ref[...]
    out_ref[...] = y


def _layer_tail(xs, mods, mixer, mix_in, w_out, ffn_w, tm, seq_len, mod_row, final, name):
    rows, d = xs.shape
    w_up, conv_w, conv_b, w_down, final_norm = ffn_w
    bf16_rows = 2 * SUBLANES
    in_specs = _halo_specs(tm, d, rows, SUBLANES)
    args = [xs] * 3
    scratch = []
    if mixer == "even":
        o, p = mix_in
        in_specs += _halo_specs(tm, o.shape[1], rows, bf16_rows) + _halo_specs(tm, p.shape[1], rows, bf16_rows)
        args += [o] * 3 + [p] * 3
    else:
        o, g, gain = mix_in
        w = o.shape[1]
        in_specs += _halo_specs(tm, w, rows, SUBLANES) + _halo_specs(tm, w, rows, bf16_rows) + [_resident((1, w))]
        args += [o] * 3 + [g] * 3 + [gain.reshape(1, w)]
        scratch = [pltpu.VMEM((tm + 2 * SUBLANES, w), BF16)]
    in_specs += [_mod_spec(d, tm, seq_len, mod_row), _resident(w_out.shape),
                 _resident(w_up.shape), _resident(conv_w.shape), _resident((1, conv_b.shape[0])),
                 _resident(w_down.shape), _resident((1, d))]
    args += [mods, w_out, w_up, conv_w, conv_b.reshape(1, -1), w_down, final_norm.reshape(1, d)]
    return pl.pallas_call(
        functools.partial(_layer_tail_kernel, mixer=mixer, tiles_per_seq=seq_len // tm, final=final),
        grid=(rows // tm,),
        in_specs=in_specs,
        out_specs=pl.BlockSpec((tm, d), lambda i: (i, 0)),
        out_shape=jax.ShapeDtypeStruct((rows, d), F32),
        scratch_shapes=scratch,
        compiler_params=_params(1),
        name=name,
    )(*args)


def _inproj_odd_kernel(x_ref, mod_ref, w_ref, *out_refs):
    m = mod_ref[0]
    h = _modnorm(x_ref[...], m[0:1], m[1:2]).astype(BF16)
    w = out_refs[0].shape[1]
    step = 512
    for c0 in range(0, w, step):
        sl = slice(c0, c0 + step)
        for n, ref in enumerate(out_refs):
            p = _dot(h, w_ref[:, n * w + c0:n * w + c0 + step])
            if n == 0:
                p = p * _sigmoid(p)
            ref[:, sl] = p.astype(ref.dtype)


def _inproj_odd(xs, mods, w_in, tm, seq_len, mod_row, want_gate, name):
    rows, d = xs.shape
    w = w_in.shape[1] // 5
    dtypes = [BF16, BF16, F32, F32] + ([BF16] if want_gate else [])
    return pl.pallas_call(
        _inproj_odd_kernel,
        grid=(rows // tm,),
        in_specs=[pl.BlockSpec((tm, d), lambda i: (i, 0)), _mod_spec(d, tm, seq_len, mod_row),
                  _resident(w_in.shape)],
        out_specs=[pl.BlockSpec((tm, w), lambda i: (i, 0))] * len(dtypes),
        out_shape=[jax.ShapeDtypeStruct((rows, w), dt) for dt in dtypes],
        compiler_params=_params(1),
        name=name,
    )(xs, mods, w_in)


def _cumsum_rows(tri, x):
    w = x.shape[1]
    hi = x.astype(BF16)
    lo = (x - hi.astype(F32)).astype(BF16)
    s = _dot(tri, jnp.concatenate([hi, lo], axis=1))
    return s[:, 0:w] + s[:, w:2 * w]


def _hgrn2_kernel(lbl_ref, qc_ref, vc_ref, zfc_ref, zbc_ref, ql_ref, vl_ref, zfl_ref, zbl_ref,
                  o_ref, ob_ref, st_ref, *, layer):
    c = HG_CHUNK
    rb = HG_BLOCK
    n_sub = rb // c
    n_levels = n_sub.bit_length() - 1
    w = o_ref.shape[1]
    heads = w // HG_HEAD_DIM
    n_ctx = qc_ref.shape[0] // rb
    n_lat = ql_ref.shape[0] // rb
    r_i = lax.broadcasted_iota(jnp.int32, (rb, rb), 0)
    c_i = lax.broadcasted_iota(jnp.int32, (rb, rb), 1)
    same_group = [(r_i // (c << l)) == (c_i // (c << l)) for l in range(n_levels)]
    chunk_id = lax.broadcasted_iota(jnp.int32, (n_sub, 1, w), 0)
    hsl = [slice(h * HG_HEAD_DIM, (h + 1) * HG_HEAD_DIM) for h in range(heads)]

    def direction_constants(d):
        fwd = d == 0
        lg = lbl_ref[d]
        e = jnp.exp(lg - jnp.max(lg, axis=0, keepdims=True))
        p = e / jnp.sum(e, axis=0, keepdims=True)
        lb = jnp.sum(p[1:layer + 1], axis=0, keepdims=True)
        ordered = (c_i <= r_i) if fwd else (c_i >= r_i)
        tri = jnp.where(ordered, 1.0, 0.0).astype(BF16)
        return lb, same_group[0] & ordered, tri

    consts = [direction_constants(0), direction_constants(1)]

    def block(d, q, v, z, want_out, out):
        fwd = d == 0
        lb, causal, tri = consts[d]
        split = lambda a, n: a.reshape(rb // n, n, w)
        merge = lambda a: a.reshape(rb, w).astype(BF16)
        f = lb + (1.0 - lb) * _sigmoid(z)
        k = 1.0 - f
        logf = jnp.log(f)
        yield
        bg = _cumsum_rows(tri, logf)
        yield
        be = bg[rb - 1:rb] if fwd else bg[0:1]
        decay = jnp.exp(be)
        if want_out:
            b0 = split(bg, c)
            r0 = b0[:, c // 2 - 1:c // 2, :] if fwd else b0[:, c // 2:c // 2 + 1, :]
            qt = split(q.astype(F32), c) * jnp.exp(b0 - r0)
            kt = split(k, c) * jnp.exp(r0 - b0)
            q_lv, k_lv = [merge(qt)], [merge(kt)]
            for l in range(1, n_levels + 1):
                g = c << l
                bl = split(bg, g)
                m = bl[:, g // 2 - 1:g // 2, :] if fwd else bl[:, g // 2:g // 2 + 1, :]
                m = jnp.broadcast_to(m[:, None], (rb // g, 1 << l, 1, w)).reshape(n_sub, 1, w)
                late_half = ((chunk_id >> (l - 1)) & 1) == 1
                q_side = late_half if fwd else ~late_half
                q_lv.append(merge(qt * jnp.where(q_side, jnp.exp(r0 - m), 0.0)))
                k_lv.append(merge(kt * jnp.where(q_side, 0.0, jnp.exp(m - r0))))
            qh = merge(qt * jnp.exp(r0))
            kh = merge(kt * jnp.exp(be - r0))
        else:
            kh = (k * jnp.exp(be - bg)).astype(BF16)
        yield
        ds = [_dot_tn(v[:, sl], kh[:, sl]) for sl in hsl]
        if want_out:
            a_lv = [[_dot_nt(ql[:, sl], kl[:, sl]) for ql, kl in zip(q_lv, k_lv)] for sl in hsl]
        yield
        st_old = []
        for h in range(heads):
            st = st_ref[d, h]
            st_old.append(st.astype(BF16))
            st_ref[d, h] = st * decay[:, hsl[h]] + ds[h]
        if want_out:
            a = []
            for lv in a_lv:
                x = lv[n_levels]
                for l in range(n_levels - 1, 0, -1):
                    x = jnp.where(same_group[l], lv[l], x)
                a.append(jnp.where(causal, lv[0], x).astype(BF16))
        yield
        if want_out:
            out[d] = jnp.concatenate([_dot(a[h], v[:, hsl[h]]) + _dot_nt(qh[:, hsl[h]], st_old[h])
                                      for h in range(heads)], axis=1)
        yield

    def run_interleaved(gens):
        for _ in zip(*gens):
            pass

    def block_rows(d, j, n):
        return pl.ds(pl.multiple_of((j if d == 0 else n - 1 - j) * rb, rb), rb)

    def ctx_step(j, carry):
        gens = []
        for d, z_ref in ((0, zfc_ref), (1, zbc_ref)):
            rows = block_rows(d, j, n_ctx)
            gens.append(block(d, qc_ref[rows, :], vc_ref[rows, :], z_ref[rows, :], False, None))
        run_interleaved(gens)
        return carry

    def lat_step(j, carry):
        out = {}
        rows = [block_rows(d, j, n_lat) for d in (0, 1)]
        gens = [block(d, ql_ref[rows[d], :], vl_ref[rows[d], :], z_ref[rows[d], :], True, out)
                for d, z_ref in ((0, zfl_ref), (1, zbl_ref))]
        run_interleaved(gens)
        o_ref[rows[0], :] = out[0]
        ob_ref[rows[1], :] = out[1]
        return carry

    st_ref[...] = jnp.zeros_like(st_ref)
    lax.fori_loop(0, n_ctx, ctx_step, 0)
    lax.fori_loop(0, n_lat, lat_step, 0, unroll=2)
    o_ref[...] += ob_ref[...]


def _hgrn2(lat_in, ctx_in, lb_logits, layer, batch, seq, ctx_len):
    w = lat_in[0].shape[1]
    ctx = pl.BlockSpec((ctx_len, HG_LANES), lambda b, g: (b, g))
    lat = pl.BlockSpec((seq, HG_LANES), lambda b, g: (b, g))
    heads = HG_LANES // HG_HEAD_DIM
    return pl.pallas_call(
        functools.partial(_hgrn2_kernel, layer=layer),
        grid=(batch, w // HG_LANES),
        in_specs=[pl.BlockSpec((2, lb_logits.shape[1], HG_LANES), lambda b, g: (0, 0, g))] + [ctx] * 4 + [lat] * 4,
        out_specs=lat,
        out_shape=jax.ShapeDtypeStruct((batch * seq, w), F32),
        scratch_shapes=[pltpu.VMEM((seq, HG_LANES), F32),
                        pltpu.VMEM((2, heads, HG_HEAD_DIM, HG_HEAD_DIM), F32)],
        compiler_params=_params(2),
        name="hgrn2_scan",
    )(lb_logits, *ctx_in, *lat_in)


def kernel(x, c, ctx, c_ctx, w_mod, b_mod, ev_w_in, ev_w_out, da_lq1, da_lk1, da_lq2, da_lk2, da_subln, pool_w,
           pool_scale, hg_w_in, hg_w_out, hg_lb_logits, hg_norm, ffn_w_up, ffn_conv_w, ffn_conv_b, ffn_w_down,
           final_norm):
    batch, seq, d = x.shape
    ctx_len = ctx.shape[1]
    depth = w_mod.shape[0]
    assert seq % TM_LAT == 0 and seq % TM_PROJ == 0 and seq % TQ == 0 and seq % HG_BLOCK == 0
    assert ctx_len % TM_CTX == 0 and ctx_len % HG_BLOCK == 0

    xl = x.reshape(batch * seq, d)
    xc = ctx.reshape(batch * ctx_len, d)
    cond_rows = 2 * SUBLANES
    cond = jnp.zeros((cond_rows, d), F32).at[:batch].set(c).at[batch].set(c_ctx)
    mods_all = _modulation(cond, w_mod, b_mod).reshape(depth, cond_rows, 6, d)
    lat = dict(tm=TM_LAT, seq_len=seq, mod_row=None)
    lat_proj = dict(tm=TM_PROJ, seq_len=seq, mod_row=None)
    cxt = dict(tm=TM_CTX, seq_len=ctx_len, mod_row=batch)

    for i in range(depth):
        last = i == depth - 1
        mods = mods_all[i]
        j = i // 2
        ffn_w = (ffn_w_up[i].astype(BF16), ffn_conv_w[i], ffn_conv_b[i], ffn_w_down[i].astype(BF16), final_norm)
        if i % 2 == 0:
            lambda_init = 0.8 - 0.6 * math.exp(-0.3 * i)
            wq = ev_w_in[j].shape[1] // 4
            w_qku = jnp.concatenate([ev_w_in[j][:, :2 * wq], ev_w_in[j][:, 3 * wq:]], axis=1).astype(BF16)
            w_vt = ev_w_in[j][:, 2 * wq:3 * wq].T.astype(BF16)
            w_out = ev_w_out[j].astype(BF16)
            pw = pool_w[j].astype(BF16)
            lqk = jnp.stack([da_lq1[j], da_lk1[j], da_lq2[j], da_lk2[j]]).astype(F32)
            ql, kl, vtl, ul = _inproj_even(xl, mods, w_qku, w_vt, _rope_tables(seq), name="even_inproj_latent",
                                           **lat_proj)
            qc, kc, vtc, uc = _inproj_even(xc, mods, w_qku, w_vt, None, name="even_inproj_context", **cxt)
            ol, oc = _diff_attention(ql, kl, vtl, qc, kc, vtc, lqk, da_subln[j], lambda_init, batch, seq, ctx_len)
            mix_lat = (ol, _pool(ul, pw, pool_scale[j], seq, "pool_latent"))
            mix_ctx = (oc, _pool(uc, pw, pool_scale[j], ctx_len, "pool_context"))
            mixer = "even"
        else:
            if not last:
                raise NotImplementedError("context readout of an HGRN2 layer that is not the last layer")
            w_in = hg_w_in[j].astype(BF16)
            *lat_in, gl = _inproj_odd(xl, mods, w_in, want_gate=True, name="odd_inproj_latent", **lat_proj)
            ctx_in = _inproj_odd(xc, mods, w_in, want_gate=False, name="odd_inproj_context", **cxt)
            o = _hgrn2(lat_in, ctx_in, hg_lb_logits.astype(F32), i, batch, seq, ctx_len)
            mix_lat, mix_ctx = (o, gl, hg_norm[j]), None
            w_out = hg_w_out[j].astype(BF16)
            mixer = "odd"
        xl = _layer_tail(xl, mods, mixer, mix_lat, w_out, ffn_w, final=last, name="layer_tail_latent", **lat)
        if not last:
            xc = _layer_tail(xc, mods, mixer, mix_ctx, w_out, ffn_w, final=False, name="layer_tail_context", **cxt)
    return xl.reshape(batch, seq, d)
```
